```python
import math
import jax, jax.numpy as jnp
from jax import lax
import numpy as np


D_MODEL = 1024
BATCH = 8
SEQ = 4096
DEPTH = 1

CHUNK = 64
N_META = 16
Q_BLOCK = 128
MIX_WIDTH = D_MODEL
HG_WIDTH = MIX_WIDTH // 2
HG_HEADS = 4
HG_DK = HG_WIDTH // HG_HEADS
SB_WIDTH = MIX_WIDTH - HG_WIDTH
SB_HEADS = 8
SB_DH = SB_WIDTH // SB_HEADS
N_GROUPS = 4
EXPERTS_PER_GROUP = 4
N_EXPERTS = N_GROUPS * EXPERTS_PER_GROUP
TOP_K_INNER = 2
D_EXPERT = D_MODEL // 2
ALPHA = (2 * DEPTH) ** 0.25
BETA = (8 * DEPTH) ** -0.25
LN_EPS = 1e-5
RMS_EPS = 1e-6
SPLITS = (HG_WIDTH, 2 * HG_WIDTH, 3 * HG_WIDTH, 4 * HG_WIDTH,
          4 * HG_WIDTH + SB_WIDTH, 4 * HG_WIDTH + 2 * SB_WIDTH)
IN_COLS = 4 * HG_WIDTH + 3 * SB_WIDTH

kernel_name = 'hymba_hgrn2_stickbreak_hmoe_deepnorm'


def layer_norm(x, g, b):
    xf = x.astype(jnp.float32)
    mu = jnp.mean(xf, axis=-1, keepdims=True)
    var = jnp.mean(jnp.square(xf - mu), axis=-1, keepdims=True)
    return ((xf - mu) * lax.rsqrt(var + LN_EPS) * g + b).astype(x.dtype)


def head_rms_norm(x, g):
    xf = x.astype(jnp.float32)
    y = xf * lax.rsqrt(jnp.mean(xf * xf, axis=-1, keepdims=True) + RMS_EPS)
    return y.reshape(*x.shape[:-2], -1) * g


def hgrn2_mixer(q_raw, f_raw, i_raw, g_raw, lb, norm_g):
    B, Lp, _ = q_raw.shape
    nc = Lp // CHUNK
    zf = f_raw.astype(jnp.float32)
    log_f = jnp.log(lb + (1.0 - lb) * jax.nn.sigmoid(zf))
    k = (1.0 - lb) * jax.nn.sigmoid(-zf)
    q = jax.nn.silu(q_raw.astype(jnp.float32))
    v = i_raw.astype(jnp.float32)

    def to_chunks(a):
        return a.reshape(B, nc, CHUNK, HG_HEADS, HG_DK).transpose(1, 0, 3, 2, 4)

    qc, kc, vc, lfc = to_chunks(q), to_chunks(k), to_chunks(v), to_chunks(log_f)
    causal = jnp.tril(jnp.ones((CHUNK, CHUNK), dtype=bool))

    def step(S, inp):
        qb, kb, vb, lfb = inp
        bcum = jnp.cumsum(lfb, axis=2)
        diff = bcum[:, :, :, None, :] - bcum[:, :, None, :, :]
        decay = jnp.exp(jnp.where(causal[:, :, None], diff, -jnp.inf))
        scores = jnp.einsum('bhtd,bhsd,bhtsd->bhts', qb, kb, decay)
        o = (jnp.einsum('bhts,bhsv->bhtv', scores, vb)
             + jnp.einsum('bhtd,bhdv->bhtv', qb * jnp.exp(bcum), S))
        b_last = bcum[:, :, -1:, :]
        S_new = (jnp.exp(b_last[:, :, 0, :, None]) * S
                 + jnp.einsum('bhsd,bhsv->bhdv', kb * jnp.exp(b_last - bcum), vb))
        return S_new, o

    S0 = jnp.zeros((B, HG_HEADS, HG_DK, HG_DK), jnp.float32)
    _, oc = lax.scan(step, S0, (qc, kc, vc, lfc))
    o = oc.transpose(1, 0, 3, 2, 4).reshape(B, Lp, HG_HEADS, HG_DK)
    gate = jax.nn.sigmoid(g_raw.astype(jnp.float32))
    return head_rms_norm(o, norm_g) * gate


def stick_breaking_mixer(q, k, v, norm_g):
    B, Lp, _ = q.shape

    def heads(a):
        return a.astype(jnp.float32).reshape(B, Lp, SB_HEADS, SB_DH).transpose(0, 2, 1, 3)

    qh = heads(q) * (SB_DH ** -0.5)
    kh, vh = heads(k), heads(v)
    outs = []
    for blk in range(Lp // Q_BLOCK):
        start = blk * Q_BLOCK
        end = start + Q_BLOCK
        z = jnp.einsum('bhqd,bhkd->bhqk', qh[:, :, start:end], kh[:, :, :end])
        mask = jnp.arange(end)[None, :] < (start + jnp.arange(Q_BLOCK))[:, None]
        log_keep = jnp.where(mask, jax.nn.log_sigmoid(-z), 0.0)
        log_a = jax.nn.log_sigmoid(z) + lax.cumsum(log_keep, axis=3, reverse=True) - log_keep
        a = jnp.where(mask, jnp.exp(log_a), 0.0)
        outs.append(jnp.einsum('bhqk,bhkd->bhqd', a, vh[:, :, :end]))
    o = jnp.concatenate(outs, axis=2).transpose(0, 2, 1, 3)
    return head_rms_norm(o, norm_g)


def hierarchical_moe(h, w_rg, b_rg, w_re, b_re, w1, w3, w2):
    B, L, D = h.shape
    xf = h.reshape(-1, D)
    logits_g = (xf @ w_rg).astype(jnp.float32) + b_rg
    probs_g = jax.nn.softmax(logits_g, axis=-1)
    _, grp = lax.top_k(logits_g, 1)
    grp_oh = jax.nn.one_hot(grp[:, 0], N_GROUPS, dtype=jnp.float32)
    p_grp = jnp.sum(probs_g * grp_oh, axis=-1, keepdims=True)
    logits_e = jnp.einsum('nd,dge->nge', xf, w_re).astype(jnp.float32) + b_re
    logits_in = jnp.einsum('nge,ng->ne', logits_e, grp_oh)
    top_v, top_i = lax.top_k(logits_in, TOP_K_INNER)
    w_inner = jax.nn.softmax(top_v, axis=-1)
    expert_id = grp * EXPERTS_PER_GROUP + top_i
    gates = p_grp * jnp.sum(jax.nn.one_hot(expert_id, N_EXPERTS, dtype=jnp.float32)
                            * w_inner[..., None], axis=1)
    y = jnp.zeros(xf.shape, jnp.float32)
    for e in range(N_EXPERTS):
        hid = jax.nn.silu(xf @ w1[e]) * (xf @ w3[e])
        y = y + gates[:, e:e + 1] * (hid @ w2[e])
    return y.astype(h.dtype).reshape(B, L, D)


def setup_inputs(seed: int = 0) -> dict:
    key = jax.random.key(seed)
    ks = jax.random.split(key, 18)
    f32 = jnp.float32
    col_scale = jnp.concatenate([
        jnp.ones((2 * HG_WIDTH,), f32), jnp.full((HG_WIDTH,), BETA, f32),
        jnp.ones((HG_WIDTH + 2 * SB_WIDTH,), f32),
        jnp.full((SB_WIDTH,), BETA, f32)])
    w_in = jax.random.normal(ks[2], (DEPTH, D_MODEL, IN_COLS), f32) * (D_MODEL ** -0.5) * col_scale
    return {
        'x': jax.random.normal(ks[0], (BATCH, SEQ, D_MODEL), f32),
        'meta_tokens': jax.random.normal(ks[1], (N_META, D_MODEL), f32),
        'w_in': w_in,
        'hg_lower_bound': 0.1 * jax.random.normal(ks[3], (DEPTH + 1, HG_WIDTH), f32),
        'hg_norm_g': 1.0 + 0.02 * jax.random.normal(ks[4], (DEPTH, HG_WIDTH), f32),
        'sb_norm_g': 1.0 + 0.02 * jax.random.normal(ks[5], (DEPTH, SB_WIDTH), f32),
        'w_out': jax.random.normal(ks[6], (DEPTH, MIX_WIDTH, D_MODEL), f32) * (MIX_WIDTH ** -0.5) * BETA,
        'ln1_g': 1.0 + 0.02 * jax.random.normal(ks[7], (DEPTH, D_MODEL), f32),
        'ln1_b': 0.02 * jax.random.normal(ks[8], (DEPTH, D_MODEL), f32),
        'w_router_group': jax.random.normal(ks[9], (DEPTH, D_MODEL, N_GROUPS), f32) * (D_MODEL ** -0.5),
        'b_router_group': 0.01 * jax.random.normal(ks[10], (DEPTH, N_GROUPS), f32),
        'w_router_expert': jax.random.normal(ks[11], (DEPTH, D_MODEL, N_GROUPS, EXPERTS_PER_GROUP), f32) * (D_MODEL ** -0.5),
        'b_router_expert': 0.01 * jax.random.normal(ks[12], (DEPTH, N_GROUPS, EXPERTS_PER_GROUP), f32),
        'w_exp_gate': jax.random.normal(ks[13], (DEPTH, N_EXPERTS, D_MODEL, D_EXPERT), f32) * (D_MODEL ** -0.5) * BETA,
        'w_exp_up': jax.random.normal(ks[14], (DEPTH, N_EXPERTS, D_MODEL, D_EXPERT), f32) * (D_MODEL ** -0.5) * BETA,
        'w_exp_down': jax.random.normal(ks[15], (DEPTH, N_EXPERTS, D_EXPERT, D_MODEL), f32) * (D_EXPERT ** -0.5) * BETA,
        'ln2_g': 1.0 + 0.02 * jax.random.normal(ks[16], (DEPTH, D_MODEL), f32),
        'ln2_b': 0.02 * jax.random.normal(ks[17], (DEPTH, D_MODEL), f32),
    }


def reference(x, meta_tokens, w_in, hg_lower_bound, hg_norm_g, sb_norm_g, w_out, ln1_g, ln1_b,
              w_router_group, b_router_group, w_router_expert, b_router_expert,
              w_exp_gate, w_exp_up, w_exp_down, ln2_g, ln2_b):
    B = x.shape[0]
    meta = jnp.broadcast_to(meta_tokens[None].astype(x.dtype), (B, N_META, D_MODEL))
    h = jnp.concatenate([meta, x], axis=1)
    L = h.shape[1]
    Lp = -(-L // Q_BLOCK) * Q_BLOCK
    lb_all = jnp.cumsum(jax.nn.softmax(hg_lower_bound.astype(jnp.float32), axis=0), axis=0)
    for layer in range(DEPTH):
        proj = jnp.einsum('bld,dp->blp', h, w_in[layer])
        proj = jnp.pad(proj, ((0, 0), (0, Lp - L), (0, 0)))
        hq, hf, hi, hg, sq, sk, sv = jnp.split(proj, SPLITS, axis=-1)
        o_hg = hgrn2_mixer(hq, hf, hi, hg, lb_all[layer], hg_norm_g[layer])
        o_sb = stick_breaking_mixer(sq, sk, sv, sb_norm_g[layer])
        mixed = jnp.concatenate([o_hg, o_sb], axis=-1)[:, :L].astype(h.dtype)
        h = layer_norm(ALPHA * h + mixed @ w_out[layer], ln1_g[layer], ln1_b[layer])
        ffn = hierarchical_moe(h, w_router_group[layer], b_router_group[layer],
                               w_router_expert[layer], b_router_expert[layer],
                               w_exp_gate[layer], w_exp_up[layer], w_exp_down[layer])
        h = layer_norm(ALPHA * h + ffn, ln2_g[layer], ln2_b[layer])
    return h[:, N_META:]
```

```python
import functools

import jax
import jax.numpy as jnp
from jax import lax
from jax.experimental import pallas as pl
from jax.experimental.pallas import tpu as pltpu

F32 = jnp.float32
BF16 = jnp.bfloat16

D_MODEL = 1024
N_META = 16
HG_WIDTH = 512
HG_HEADS = 4
HG_DK = 128
SB_WIDTH = 512
SB_DH = 64
N_GROUPS = 4
EXPERTS_PER_GROUP = 4
N_EXPERTS = 16
D_EXPERT = 512
DEPTH = 1
ALPHA = (2 * DEPTH) ** 0.25
LN_EPS = 1e-5
RMS_EPS = 1e-6
N_SPLITS = 7

LANES = 128
HG_CHUNK = 128
SB_TILE = 128
META_TILE = 128
SKIP_LOG = -100.0
VMEM_LIMIT = 56 * 1024 * 1024

NT_DIMS = (((1,), (1,)), ((), ()))


def _dot(a, b):
    return jnp.dot(a, b, preferred_element_type=F32)


def _dot_nt(a, b):
    return lax.dot_general(a, b, NT_DIMS, preferred_element_type=F32)


def _split3(x):
    hi = x.astype(BF16)
    r1 = x - hi.astype(F32)
    mid = r1.astype(BF16)
    lo = (r1 - mid.astype(F32)).astype(BF16)
    return hi, mid, lo


def _proj_kernel(x_ref, w_ref, *out_refs):
    xb = x_ref[...].astype(BF16)
    for j, o_ref in enumerate(out_refs):
        o_ref[...] = _dot(xb, w_ref[:, j * 512:(j + 1) * 512])


def _proj(x2d, w_bf16, tm):
    n = x2d.shape[0]
    out_shape = [jax.ShapeDtypeStruct((n, 512), F32) for _ in range(N_SPLITS)]
    return pl.pallas_call(
        _proj_kernel,
        grid=(n // tm,),
        in_specs=[pl.BlockSpec((tm, D_MODEL), lambda i: (i, 0)),
                  pl.BlockSpec((D_MODEL, N_SPLITS * 512), lambda i: (0, 0))],
        out_specs=[pl.BlockSpec((tm, 512), lambda i: (i, 0)) for _ in range(N_SPLITS)],
        out_shape=out_shape,
        compiler_params=pltpu.CompilerParams(dimension_semantics=("parallel",),
                                             vmem_limit_bytes=VMEM_LIMIT),
        name="proj",
    )(x2d, w_bf16)


def _hgrn_head(zf, zi, lb, st, tri, zq=None, causal=None):
    c = zf.shape[0]
    k = (1.0 - lb) * jax.nn.sigmoid(-zf)
    lf = jnp.log(1.0 - k)
    h, m, l = _split3(lf)
    bc = _dot(tri, h) + _dot(tri, m) + _dot(tri, l)
    b_last = bc[c - 1:c, :]
    o = None
    if zq is not None:
        q = zq * jax.nn.sigmoid(zq)
        ref = bc[c // 2 - 1:c // 2, :]
        qe = (q * jnp.exp(bc - ref)).astype(BF16)
        ke = (k * jnp.exp(ref - bc)).astype(BF16)
        sc = jnp.where(causal, _dot_nt(qe, ke), 0.0)
        o = _dot(sc.astype(BF16), zi.astype(BF16))
        o = o + _dot_nt((q * jnp.exp(bc)).astype(BF16), st.astype(BF16))
    kl = (k * jnp.exp(b_last - bc)).astype(BF16)
    st_new = st * jnp.exp(b_last) + _dot(zi.T.astype(BF16), kl)
    return st_new, o


def _hgrn_kernel(q_ref, f_ref, i_ref, g_ref, mf_ref, mi_ref, lb_ref, ng_ref, o_ref, st_ref):
    c = HG_CHUNK
    row = lax.broadcasted_iota(jnp.int32, (c, c), 0)
    col = lax.broadcasted_iota(jnp.int32, (c, c), 1)
    causal = col <= row
    tri = causal.astype(BF16)

    @pl.when(pl.program_id(1) == 0)
    def _():
        for h in range(HG_HEADS):
            sl = slice(h * HG_DK, (h + 1) * HG_DK)
            st0 = jnp.zeros((HG_DK, HG_DK), F32)
            st_ref[h], _ = _hgrn_head(mf_ref[:, sl], mi_ref[:, sl], lb_ref[:, sl], st0, tri)

    for h in range(HG_HEADS):
        sl = slice(h * HG_DK, (h + 1) * HG_DK)
        st_new, o = _hgrn_head(f_ref[:, sl], i_ref[:, sl], lb_ref[:, sl], st_ref[h], tri,
                               zq=q_ref[:, sl], causal=causal)
        st_ref[h] = st_new
        ms = jnp.mean(o * o, axis=-1, keepdims=True)
        y = o * lax.rsqrt(ms + RMS_EPS) * ng_ref[:, sl] * jax.nn.sigmoid(g_ref[:, sl])
        o_ref[:, sl] = y.astype(o_ref.dtype)


def _hgrn(hq, hf, hi, hg, mf, mi, lb, ng, batch, seq):
    nc = seq // HG_CHUNK
    blk = pl.BlockSpec((HG_CHUNK, HG_WIDTH), lambda b, c: (b * nc + c, 0))
    meta = pl.BlockSpec((META_TILE, HG_WIDTH), lambda b, c: (0, 0))
    vec = pl.BlockSpec((1, HG_WIDTH), lambda b, c: (0, 0))
    return pl.pallas_call(
        _hgrn_kernel,
        grid=(batch, nc),
        in_specs=[blk, blk, blk, blk, meta, meta, vec, vec],
        out_specs=blk,
        out_shape=jax.ShapeDtypeStruct((batch * seq, HG_WIDTH), BF16),
        scratch_shapes=[pltpu.VMEM((HG_HEADS, HG_DK, HG_DK), F32)],
        compiler_params=pltpu.CompilerParams(dimension_semantics=("parallel", "arbitrary"),
                                             vmem_limit_bytes=VMEM_LIMIT),
        name="hgrn2",
    )(hq, hf, hi, hg, mf, mi, lb, ng)


def _sb_tile(q_heads, k, v_heads, rev, carries, mask):
    kb = k.astype(BF16)
    contrib = None
    new_carries = []
    for qh, vh, carry in zip(q_heads, v_heads, carries):
        z = _dot_nt(qh, kb)
        sp = jnp.log(1.0 + jnp.exp(-jnp.abs(z)))
        ls_pos = jnp.minimum(z, 0.0) - sp
        lk = jnp.minimum(-z, 0.0) - sp
        if mask is not None:
            lk = jnp.where(mask, lk, 0.0)
        h, m, l = _split3(lk)
        cs = _dot(h, rev) + _dot(m, rev) + _dot(l, rev)
        a = jnp.exp(ls_pos + cs + carry)
        if mask is not None:
            a = jnp.where(mask, a, 0.0)
        new_carries.append(carry + jnp.sum(lk, axis=-1, keepdims=True))
        part = _dot(a.astype(BF16), vh)
        contrib = part if contrib is None else contrib + part
    return contrib, new_carries


def _sb_kernel(q_ref, k_ref, v_ref, mk_ref, mv_ref, ng_ref, o_ref, acc_ref, car_ref):
    t = SB_TILE
    qi = pl.program_id(2)
    lane = lax.broadcasted_iota(jnp.int32, (1, LANES), 1)
    lo = lane < SB_DH
    row = lax.broadcasted_iota(jnp.int32, (t, t), 0)
    col = lax.broadcasted_iota(jnp.int32, (t, t), 1)
    rev = (row > col).astype(BF16)

    q = q_ref[...] * (SB_DH ** -0.5)
    q_heads = (jnp.where(lo, q, 0.0).astype(BF16), jnp.where(lo, 0.0, q).astype(BF16))

    def heads_of(v):
        return (jnp.where(lo, v, 0.0).astype(BF16), jnp.where(lo, 0.0, v).astype(BF16))

    def alive_of(carries):
        return jnp.max(jnp.maximum(carries[0], carries[1])) > SKIP_LOG

    zero = jnp.zeros((t, 1), F32)
    start = pl.multiple_of(qi * t, t)
    acc, carries = _sb_tile(q_heads, k_ref[pl.ds(start, t), :], heads_of(v_ref[pl.ds(start, t), :]),
                            rev, (zero, zero), col < row)

    def cond(state):
        return jnp.logical_and(state[0] >= 0, state[1])

    def body(state):
        kt, _, acc, c0, c1 = state
        s = pl.multiple_of(kt * t, t)
        part, cs = _sb_tile(q_heads, k_ref[pl.ds(s, t), :], heads_of(v_ref[pl.ds(s, t), :]),
                            rev, (c0, c1), None)
        return kt - 1, alive_of(cs), acc + part, cs[0], cs[1]

    _, alive, acc, c0, c1 = lax.while_loop(cond, body, (qi - 1, alive_of(carries), acc, carries[0], carries[1]))
    acc_ref[...] = acc
    car_ref[0] = c0
    car_ref[1] = c1

    @pl.when(alive)
    def _():
        part, _ = _sb_tile(q_heads, mk_ref[...], heads_of(mv_ref[...]), rev,
                           (car_ref[0], car_ref[1]), col >= META_TILE - N_META)
        acc_ref[...] += part

    o = acc_ref[...]
    sq = o * o
    ms_lo = jnp.sum(jnp.where(lo, sq, 0.0), axis=-1, keepdims=True) * (1.0 / SB_DH)
    ms_hi = jnp.sum(jnp.where(lo, 0.0, sq), axis=-1, keepdims=True) * (1.0 / SB_DH)
    inv = jnp.where(lo, lax.rsqrt(ms_lo + RMS_EPS), lax.rsqrt(ms_hi + RMS_EPS))
    o_ref[...] = (o * inv * ng_ref[...]).astype(o_ref.dtype)


def _stickbreak(sq, sk, sv, mk, mv, ng, batch, seq):
    nq = seq // SB_TILE
    pairs = SB_WIDTH // LANES
    qblk = pl.BlockSpec((SB_TILE, LANES), lambda b, p, i: (b * nq + i, p))
    kvblk = pl.BlockSpec((seq, LANES), lambda b, p, i: (b, p))
    mblk = pl.BlockSpec((META_TILE, LANES), lambda b, p, i: (0, p))
    gblk = pl.BlockSpec((1, LANES), lambda b, p, i: (0, p))
    return pl.pallas_call(
        _sb_kernel,
        grid=(batch, pairs, nq),
        in_specs=[qblk, kvblk, kvblk, mblk, mblk, gblk],
        out_specs=qblk,
        out_shape=jax.ShapeDtypeStruct((batch * seq, SB_WIDTH), BF16),
        scratch_shapes=[pltpu.VMEM((SB_TILE, LANES), F32), pltpu.VMEM((2, SB_TILE, 1), F32)],
        compiler_params=pltpu.CompilerParams(dimension_semantics=("parallel", "parallel", "arbitrary"),
                                             vmem_limit_bytes=VMEM_LIMIT),
        name="stickbreak",
    )(sq, sk, sv, mk, mv, ng)


def _layer_norm(x, g, b):
    mu = jnp.mean(x, axis=-1, keepdims=True)
    xc = x - mu
    var = jnp.mean(xc * xc, axis=-1, keepdims=True)
    return xc * lax.rsqrt(var + LN_EPS) * g + b


def _argmax_first(rows):
    best = rows[0]
    idx = jnp.zeros(best.shape, jnp.int32)
    for j in range(1, len(rows)):
        better = rows[j] > best
        idx = jnp.where(better, j, idx)
        best = jnp.where(better, rows[j], best)
    return best, idx


def _outproj_kernel(x_ref, ohg_ref, osb_ref, w_ref, g_ref, b_ref, wr_ref, br_ref, h_ref, gate_ref):
    y = _dot(ohg_ref[...], w_ref[:HG_WIDTH, :]) + _dot(osb_ref[...], w_ref[HG_WIDTH:, :])
    h1 = _layer_norm(ALPHA * x_ref[...] + y, g_ref[...], b_ref[...])
    h_ref[...] = h1

    h_hi, h_mid, h_lo = _split3(h1)
    w_hi, w_mid, w_lo = _split3(wr_ref[...])
    lt = (_dot_nt(w_hi, h_hi) + _dot_nt(w_hi, h_mid) + _dot_nt(w_mid, h_hi)
          + _dot_nt(w_hi, h_lo) + _dot_nt(w_lo, h_hi) + _dot_nt(w_mid, h_mid)) + br_ref[...]

    lg = [lt[g:g + 1, :] for g in range(N_GROUPS)]
    best, grp = _argmax_first(lg)
    denom = jnp.exp(lg[0] - best)
    for g in range(1, N_GROUPS):
        denom = denom + jnp.exp(lg[g] - best)
    p_grp = 1.0 / denom

    li = []
    for e in range(EXPERTS_PER_GROUP):
        acc = jnp.zeros_like(best)
        for g in range(N_GROUPS):
            r = N_GROUPS + g * EXPERTS_PER_GROUP + e
            acc = acc + jnp.where(grp == g, lt[r:r + 1, :], 0.0)
        li.append(acc)
    v1, i1 = _argmax_first(li)
    neg = jnp.full_like(v1, -jnp.inf)
    v2, i2 = _argmax_first([jnp.where(i1 == e, neg, li[e]) for e in range(EXPERTS_PER_GROUP)])
    ex = jnp.exp(v2 - v1)
    w1 = 1.0 / (1.0 + ex)
    w2 = ex / (1.0 + ex)

    gate_ref[...] = jnp.zeros(gate_ref.shape, F32)
    for ge in range(N_EXPERTS):
        g, e = divmod(ge, EXPERTS_PER_GROUP)
        inner = jnp.where(i1 == e, w1, jnp.where(i2 == e, w2, 0.0))
        gate_ref[ge:ge + 1, :] = jnp.where(grp == g, p_grp * inner, 0.0)


def _outproj(x2d, ohg, osb, w_bf16, g, b, wr, br, tm):
    n = x2d.shape[0]
    const = lambda shape: pl.BlockSpec(shape, lambda i: (0, 0))
    return pl.pallas_call(
        _outproj_kernel,
        grid=(n // tm,),
        in_specs=[pl.BlockSpec((tm, D_MODEL), lambda i: (i, 0)),
                  pl.BlockSpec((tm, HG_WIDTH), lambda i: (i, 0)),
                  pl.BlockSpec((tm, SB_WIDTH), lambda i: (i, 0)),
                  const((D_MODEL, D_MODEL)), const((1, D_MODEL)), const((1, D_MODEL)),
                  const((32, D_MODEL)), const((32, 1))],
        out_specs=[pl.BlockSpec((tm, D_MODEL), lambda i: (i, 0)),
                   pl.BlockSpec((LANES, tm), lambda i: (0, i))],
        out_shape=[jax.ShapeDtypeStruct((n, D_MODEL), F32),
                   jax.ShapeDtypeStruct((LANES, n), F32)],
        compiler_params=pltpu.CompilerParams(dimension_semantics=("parallel",),
                                             vmem_limit_bytes=VMEM_LIMIT),
        name="outproj",
    )(x2d, ohg, osb, w_bf16, g, b, wr, br)


def _moe_kernel(h_ref, gate_ref, w1_ref, w3_ref, w2_ref, g_ref, b_ref, o_ref, acc_ref):
    e = pl.program_id(1)

    @pl.when(e == 0)
    def _():
        acc_ref[...] = jnp.zeros(acc_ref.shape, F32)

    hb = h_ref[...].astype(BF16)
    a = _dot(hb, w1_ref[0])
    hid = a * jax.nn.sigmoid(a) * _dot(hb, w3_ref[0])
    gates = gate_ref[...].T
    lane = lax.broadcasted_iota(jnp.int32, (1, LANES), 1)
    ge = jnp.sum(jnp.where(lane == e, gates, 0.0), axis=-1, keepdims=True)
    acc_ref[...] += ge * _dot(hid.astype(BF16), w2_ref[0])

    @pl.when(e == N_EXPERTS - 1)
    def _():
        o_ref[...] = _layer_norm(ALPHA * h_ref[...] + acc_ref[...], g_ref[...], b_ref[...])


def _moe(h1, gates_t, w1, w3, w2, g, b, tm):
    n = h1.shape[0]
    return pl.pallas_call(
        _moe_kernel,
        grid=(n // tm, N_EXPERTS),
        in_specs=[pl.BlockSpec((tm, D_MODEL), lambda i, e: (i, 0)),
                  pl.BlockSpec((LANES, tm), lambda i, e: (0, i)),
                  pl.BlockSpec((1, D_MODEL, D_EXPERT), lambda i, e: (e, 0, 0)),
                  pl.BlockSpec((1, D_MODEL, D_EXPERT), lambda i, e: (e, 0, 0)),
                  pl.BlockSpec((1, D_EXPERT, D_MODEL), lambda i, e: (e, 0, 0)),
                  pl.BlockSpec((1, D_MODEL), lambda i, e: (0, 0)),
                  pl.BlockSpec((1, D_MODEL), lambda i, e: (0, 0))],
        out_specs=pl.BlockSpec((tm, D_MODEL), lambda i, e: (i, 0)),
        out_shape=jax.ShapeDtypeStruct((n, D_MODEL), F32),
        scratch_shapes=[pltpu.VMEM((tm, D_MODEL), F32)],
        compiler_params=pltpu.CompilerParams(dimension_semantics=("parallel", "arbitrary"),
                                             vmem_limit_bytes=VMEM_LIMIT),
        name="moe",
    )(h1, gates_t, w1, w3, w2, g, b)


def kernel(x, meta_tokens, w_in, hg_lower_bound, hg_norm_g, sb_norm_g, w_out, ln1_g, ln1_b,
           w_router_group, b_router_group, w_router_expert, b_router_expert,
           w_exp_gate, w_exp_up, w_exp_down, ln2_g, ln2_b):
    batch, seq, d = x.shape
    assert d == D_MODEL and seq % HG_CHUNK == 0 and seq % SB_TILE == 0
    assert w_in.shape[0] == DEPTH == 1
    n = batch * seq
    x2d = x.reshape(n, d)

    w_in_b = w_in[0].astype(BF16)
    w_out_b = w_out[0].astype(BF16)
    lb = jnp.cumsum(jax.nn.softmax(hg_lower_bound.astype(F32), axis=0), axis=0)[0].reshape(1, HG_WIDTH)
    wr = jnp.concatenate([w_router_group[0].T,
                          w_router_expert[0].reshape(d, N_EXPERTS).T,
                          jnp.zeros((32 - N_GROUPS - N_EXPERTS, d), F32)], axis=0)
    br = jnp.concatenate([b_router_group[0], b_router_expert[0].reshape(N_EXPERTS),
                          jnp.zeros((32 - N_GROUPS - N_EXPERTS,), F32)]).reshape(32, 1)
    meta_pad = jnp.concatenate([jnp.zeros((META_TILE - N_META, d), F32), meta_tokens.astype(F32)], axis=0)

    hq, hf, hi, hg, sq, sk, sv = _proj(x2d, w_in_b, 512)
    _, mf, mi, _, _, mk, mv = _proj(meta_pad, w_in_b, META_TILE)

    o_hg = _hgrn(hq, hf, hi, hg, mf, mi, lb, hg_norm_g[0].reshape(1, HG_WIDTH), batch, seq)
    o_sb = _stickbreak(sq, sk, sv, mk, mv, sb_norm_g[0].reshape(1, SB_WIDTH), batch, seq)

    h1, gates_t = _outproj(x2d, o_hg, o_sb, w_out_b, ln1_g[0].reshape(1, d), ln1_b[0].reshape(1, d),
                           wr, br, 512)
    out = _moe(h1, gates_t, w_exp_gate[0].astype(BF16), w_exp_up[0].astype(BF16),
               w_exp_down[0].astype(BF16), ln2_g[0].reshape(1, d), ln2_b[0].reshape(1, d), 1024)
    return out.reshape(batch, seq, d)
```

```python
import functools

import jax
import jax.numpy as jnp
from jax import lax
from jax.experimental import pallas as pl
from jax.experimental.pallas import tpu as pltpu

F32 = jnp.float32
BF16 = jnp.bfloat16

D_MODEL = 1024
N_META = 16
HG_WIDTH = 512
HG_HEADS = 4
HG_DK = 128
SB_WIDTH = 512
SB_DH = 64
SB_PAIRS = 4
N_GROUPS = 4
EXPERTS_PER_GROUP = 4
N_EXPERTS = 16
D_EXPERT = 512
DEPTH = 1
ALPHA = (2 * DEPTH) ** 0.25
LN_EPS = 1e-5
RMS_EPS = 1e-6
N_SPLITS = 7
PROJ_DTYPES = (F32, F32, F32, F32, BF16, BF16, BF16)

LANES = 128
HG_CHUNK = 128
SB_TILE = 128
META_TILE = 128
SKIP_LOG = -88.0
VMEM_LIMIT = 56 * 1024 * 1024

NT_DIMS = (((1,), (1,)), ((), ()))


def _dot(a, b):
    return jnp.dot(a, b, preferred_element_type=F32)


def _dot_nt(a, b):
    return lax.dot_general(a, b, NT_DIMS, preferred_element_type=F32)


def _split3(x):
    hi = x.astype(BF16)
    r1 = x - hi.astype(F32)
    mid = r1.astype(BF16)
    lo = (r1 - mid.astype(F32)).astype(BF16)
    return hi, mid, lo


def _split2(x):
    hi = x.astype(BF16)
    return hi, (x - hi.astype(F32)).astype(BF16)


def _proj_kernel(x_ref, w_ref, *out_refs):
    xb = x_ref[...].astype(BF16)
    for j, o_ref in enumerate(out_refs):
        o_ref[...] = _dot(xb, w_ref[:, j * 512:(j + 1) * 512]).astype(o_ref.dtype)


def _proj(x2d, w_bf16, tm):
    n = x2d.shape[0]
    out_shape = [jax.ShapeDtypeStruct((n, 512), dt) for dt in PROJ_DTYPES]
    return pl.pallas_call(
        _proj_kernel,
        grid=(n // tm,),
        in_specs=[pl.BlockSpec((tm, D_MODEL), lambda i: (i, 0)),
                  pl.BlockSpec((D_MODEL, N_SPLITS * 512), lambda i: (0, 0))],
        out_specs=[pl.BlockSpec((tm, 512), lambda i: (i, 0)) for _ in range(N_SPLITS)],
        out_shape=out_shape,
        compiler_params=pltpu.CompilerParams(dimension_semantics=("parallel",),
                                             vmem_limit_bytes=VMEM_LIMIT),
        name="proj",
    )(x2d, w_bf16)


def _hgrn_chunk(zfs, zis, lbs, sts, tri, zqs=None, causal=None):
    n = len(zfs)
    c = zfs[0].shape[0]
    ks = [(1.0 - lbs[h]) * jax.nn.sigmoid(-zfs[h]) for h in range(n)]
    splits = [_split3(jnp.log(1.0 - k)) for k in ks]
    bcs = [_dot(tri, hi) + _dot(tri, mid) + _dot(tri, lo) for hi, mid, lo in splits]
    b_last = [bc[c - 1:c, :] for bc in bcs]
    os = None
    if zqs is not None:
        qs = [zq * jax.nn.sigmoid(zq) for zq in zqs]
        refs = [bc[c // 2 - 1:c // 2, :] for bc in bcs]
        qes = [(qs[h] * jnp.exp(bcs[h] - refs[h])).astype(BF16) for h in range(n)]
        kes = [(ks[h] * jnp.exp(refs[h] - bcs[h])).astype(BF16) for h in range(n)]
        qbs = [(qs[h] * jnp.exp(bcs[h])).astype(BF16) for h in range(n)]
        scs = [jnp.where(causal, _dot_nt(qes[h], kes[h]), 0.0).astype(BF16) for h in range(n)]
        os = [_dot(scs[h], zis[h].astype(BF16)) + _dot_nt(qbs[h], sts[h].astype(BF16)) for h in range(n)]
    kls = [(ks[h] * jnp.exp(b_last[h] - bcs[h])).astype(BF16) for h in range(n)]
    new_sts = [sts[h] * jnp.exp(b_last[h]) + _dot(zis[h].T.astype(BF16), kls[h]) for h in range(n)]
    return new_sts, os


def _hgrn_kernel(q_ref, f_ref, i_ref, g_ref, mf_ref, mi_ref, lb_ref, ng_ref, o_ref, st_ref):
    c = HG_CHUNK
    row = lax.broadcasted_iota(jnp.int32, (c, c), 0)
    col = lax.broadcasted_iota(jnp.int32, (c, c), 1)
    causal = col <= row
    tri = causal.astype(BF16)
    sls = [slice(h * HG_DK, (h + 1) * HG_DK) for h in range(HG_HEADS)]
    lbs = [lb_ref[:, sl] for sl in sls]

    @pl.when(pl.program_id(1) == 0)
    def _():
        zero = jnp.zeros((HG_DK, HG_DK), F32)
        sts, _ = _hgrn_chunk([mf_ref[:, sl] for sl in sls], [mi_ref[:, sl] for sl in sls], lbs,
                             [zero] * HG_HEADS, tri)
        for h in range(HG_HEADS):
            st_ref[h] = sts[h]

    sts, os = _hgrn_chunk([f_ref[:, sl] for sl in sls], [i_ref[:, sl] for sl in sls], lbs,
                          [st_ref[h] for h in range(HG_HEADS)], tri,
                          zqs=[q_ref[:, sl] for sl in sls], causal=causal)
    for h in range(HG_HEADS):
        st_ref[h] = sts[h]
    for h, sl in enumerate(sls):
        ms = jnp.mean(os[h] * os[h], axis=-1, keepdims=True)
        y = os[h] * lax.rsqrt(ms + RMS_EPS) * ng_ref[:, sl] * jax.nn.sigmoid(g_ref[:, sl])
        o_ref[:, sl] = y.astype(o_ref.dtype)


def _hgrn(hq, hf, hi, hg, mf, mi, lb, ng, batch, seq):
    nc = seq // HG_CHUNK
    blk = pl.BlockSpec((HG_CHUNK, HG_WIDTH), lambda b, c: (b * nc + c, 0))
    meta = pl.BlockSpec((META_TILE, HG_WIDTH), lambda b, c: (0, 0))
    vec = pl.BlockSpec((1, HG_WIDTH), lambda b, c: (0, 0))
    return pl.pallas_call(
        _hgrn_kernel,
        grid=(batch, nc),
        in_specs=[blk, blk, blk, blk, meta, meta, vec, vec],
        out_specs=blk,
        out_shape=jax.ShapeDtypeStruct((batch * seq, HG_WIDTH), BF16),
        scratch_shapes=[pltpu.VMEM((HG_HEADS, HG_DK, HG_DK), F32)],
        compiler_params=pltpu.CompilerParams(dimension_semantics=("parallel", "arbitrary"),
                                             vmem_limit_bytes=VMEM_LIMIT),
        name="hgrn2",
    )(hq, hf, hi, hg, mf, mi, lb, ng)


def _sb_tiles(qs, ks, vs, rev, carries, mask):
    n = len(qs)
    zs = [_dot_nt(qs[p], ks[p]) for p in range(n)]
    ls_pos, lks = [], []
    for z in zs:
        sp = jnp.log(1.0 + jnp.exp(-jnp.abs(z)))
        ls_pos.append(jnp.minimum(z, 0.0) - sp)
        lk = jnp.minimum(-z, 0.0) - sp
        lks.append(lk if mask is None else jnp.where(mask, lk, 0.0))
    splits = [_split2(lk) for lk in lks]
    css = [_dot(hi, rev) + _dot(lo, rev) for hi, lo in splits]
    outs = []
    for p in range(n):
        a = jnp.exp(ls_pos[p] + css[p] + carries[p])
        if mask is not None:
            a = jnp.where(mask, a, 0.0)
        outs.append(a.astype(BF16))
    parts = [_dot(outs[p], vs[p]) for p in range(n)]
    new_carries = [carries[p] + jnp.sum(lks[p], axis=-1, keepdims=True) for p in range(n)]
    return parts, new_carries


def _sb_kernel(q_ref, k_ref, v_ref, mk_ref, mv_ref, ng_ref, o_ref, acc_ref, car_ref):
    t = SB_TILE
    qi = pl.program_id(1)
    lane = lax.broadcasted_iota(jnp.int32, (1, LANES), 1)
    lo = lane < SB_DH
    row = lax.broadcasted_iota(jnp.int32, (2 * t, t), 0)
    col = lax.broadcasted_iota(jnp.int32, (2 * t, t), 1)
    rev = (lax.broadcasted_iota(jnp.int32, (t, t), 0)
           > lax.broadcasted_iota(jnp.int32, (t, t), 1)).astype(BF16)
    diag_mask = col < jnp.where(row >= t, row - t, row)
    meta_mask = col >= META_TILE - N_META

    def pair(ref, s, p):
        return ref[pl.ds(s, t), p * LANES:(p + 1) * LANES]

    qs = []
    for p in range(SB_PAIRS):
        q = q_ref[:, p * LANES:(p + 1) * LANES] * (SB_DH ** -0.5)
        qs.append(jnp.concatenate([jnp.where(lo, q, 0.0), jnp.where(lo, 0.0, q)], axis=0).astype(BF16))

    def sweep(k_of, v_of, carries, mask):
        return _sb_tiles(qs, [k_of(p) for p in range(SB_PAIRS)], [v_of(p) for p in range(SB_PAIRS)],
                         rev, carries, mask)

    def alive_of(carries):
        m = carries[0]
        for c in carries[1:]:
            m = jnp.maximum(m, c)
        return jnp.max(m) > SKIP_LOG

    zero = jnp.zeros((2 * t, 1), F32)
    start = pl.multiple_of(qi * t, t)
    accs, carries = sweep(lambda p: pair(k_ref, start, p), lambda p: pair(v_ref, start, p),
                          [zero] * SB_PAIRS, diag_mask)

    def cond(state):
        return jnp.logical_and(state[0] >= 0, state[1])

    def body(state):
        kt, _, accs, carries = state
        s = pl.multiple_of(kt * t, t)
        parts, carries = sweep(lambda p: pair(k_ref, s, p), lambda p: pair(v_ref, s, p), carries, None)
        return kt - 1, alive_of(carries), [a + b for a, b in zip(accs, parts)], carries

    _, alive, accs, carries = lax.while_loop(cond, body, (qi - 1, alive_of(carries), accs, carries))
    for p in range(SB_PAIRS):
        acc_ref[p] = accs[p]
        car_ref[p] = carries[p]

    @pl.when(alive)
    def _():
        parts, _ = sweep(lambda p: mk_ref[:, p * LANES:(p + 1) * LANES],
                         lambda p: mv_ref[:, p * LANES:(p + 1) * LANES],
                         [car_ref[p] for p in range(SB_PAIRS)], meta_mask)
        for p in range(SB_PAIRS):
            acc_ref[p] += parts[p]

    for p in range(SB_PAIRS):
        sl = slice(p * LANES, (p + 1) * LANES)
        o = jnp.where(lo, acc_ref[p, :t, :], acc_ref[p, t:, :])
        sq = o * o
        ms_lo = jnp.sum(jnp.where(lo, sq, 0.0), axis=-1, keepdims=True) * (1.0 / SB_DH)
        ms_hi = jnp.sum(jnp.where(lo, 0.0, sq), axis=-1, keepdims=True) * (1.0 / SB_DH)
        inv = jnp.where(lo, lax.rsqrt(ms_lo + RMS_EPS), lax.rsqrt(ms_hi + RMS_EPS))
        o_ref[:, sl] = (o * inv * ng_ref[:, sl]).astype(o_ref.dtype)


def _stickbreak(sq, sk, sv, mk, mv, ng, batch, seq):
    nq = seq // SB_TILE
    qblk = pl.BlockSpec((SB_TILE, SB_WIDTH), lambda b, i: (b * nq + i, 0))
    kvblk = pl.BlockSpec((seq, SB_WIDTH), lambda b, i: (b, 0))
    mblk = pl.BlockSpec((META_TILE, SB_WIDTH), lambda b, i: (0, 0))
    gblk = pl.BlockSpec((1, SB_WIDTH), lambda b, i: (0, 0))
    return pl.pallas_call(
        _sb_kernel,
        grid=(batch, nq),
        in_specs=[qblk, kvblk, kvblk, mblk, mblk, gblk],
        out_specs=qblk,
        out_shape=jax.ShapeDtypeStruct((batch * seq, SB_WIDTH), BF16),
        scratch_shapes=[pltpu.VMEM((SB_PAIRS, 2 * SB_TILE, LANES), F32),
                        pltpu.VMEM((SB_PAIRS, 2 * SB_TILE, 1), F32)],
        compiler_params=pltpu.CompilerParams(dimension_semantics=("parallel", "arbitrary"),
                                             vmem_limit_bytes=VMEM_LIMIT),
        name="stickbreak",
    )(sq, sk, sv, mk, mv, ng)


def _layer_norm(x, g, b):
    mu = jnp.mean(x, axis=-1, keepdims=True)
    xc = x - mu
    var = jnp.mean(xc * xc, axis=-1, keepdims=True)
    return xc * lax.rsqrt(var + LN_EPS) * g + b


def _argmax_first(rows):
    best = rows[0]
    idx = jnp.zeros(best.shape, jnp.int32)
    for j in range(1, len(rows)):
        better = rows[j] > best
        idx = jnp.where(better, j, idx)
        best = jnp.where(better, rows[j], best)
    return best, idx


def _outproj_kernel(x_ref, ohg_ref, osb_ref, w_ref, g_ref, b_ref, wr_ref, br_ref, h_ref, gate_ref):
    y = _dot(ohg_ref[...], w_ref[:HG_WIDTH, :]) + _dot(osb_ref[...], w_ref[HG_WIDTH:, :])
    h1 = _layer_norm(ALPHA * x_ref[...] + y, g_ref[...], b_ref[...])
    h_ref[...] = h1

    h_hi, h_mid, h_lo = _split3(h1)
    w_hi, w_mid, w_lo = _split3(wr_ref[...])
    lt = (_dot_nt(w_hi, h_hi) + _dot_nt(w_hi, h_mid) + _dot_nt(w_mid, h_hi)
          + _dot_nt(w_hi, h_lo) + _dot_nt(w_lo, h_hi) + _dot_nt(w_mid, h_mid)) + br_ref[...]

    lg = [lt[g:g + 1, :] for g in range(N_GROUPS)]
    best, grp = _argmax_first(lg)
    denom = jnp.exp(lg[0] - best)
    for g in range(1, N_GROUPS):
        denom = denom + jnp.exp(lg[g] - best)
    p_grp = 1.0 / denom

    li = []
    for e in range(EXPERTS_PER_GROUP):
        acc = jnp.zeros_like(best)
        for g in range(N_GROUPS):
            r = N_GROUPS + g * EXPERTS_PER_GROUP + e
            acc = acc + jnp.where(grp == g, lt[r:r + 1, :], 0.0)
        li.append(acc)
    v1, i1 = _argmax_first(li)
    neg = jnp.full_like(v1, -jnp.inf)
    v2, i2 = _argmax_first([jnp.where(i1 == e, neg, li[e]) for e in range(EXPERTS_PER_GROUP)])
    ex = jnp.exp(v2 - v1)
    w1 = 1.0 / (1.0 + ex)
    w2 = ex / (1.0 + ex)

    gate_ref[...] = jnp.zeros(gate_ref.shape, F32)
    for ge in range(N_EXPERTS):
        g, e = divmod(ge, EXPERTS_PER_GROUP)
        inner = jnp.where(i1 == e, w1, jnp.where(i2 == e, w2, 0.0))
        gate_ref[ge:ge + 1, :] = jnp.where(grp == g, p_grp * inner, 0.0)


def _outproj(x2d, ohg, osb, w_bf16, g, b, wr, br, tm):
    n = x2d.shape[0]
    const = lambda shape: pl.BlockSpec(shape, lambda i: (0, 0))
    return pl.pallas_call(
        _outproj_kernel,
        grid=(n // tm,),
        in_specs=[pl.BlockSpec((tm, D_MODEL), lambda i: (i, 0)),
                  pl.BlockSpec((tm, HG_WIDTH), lambda i: (i, 0)),
                  pl.BlockSpec((tm, SB_WIDTH), lambda i: (i, 0)),
                  const((D_MODEL, D_MODEL)), const((1, D_MODEL)), const((1, D_MODEL)),
                  const((32, D_MODEL)), const((32, 1))],
        out_specs=[pl.BlockSpec((tm, D_MODEL), lambda i: (i, 0)),
                   pl.BlockSpec((LANES, tm), lambda i: (0, i))],
        out_shape=[jax.ShapeDtypeStruct((n, D_MODEL), F32),
                   jax.ShapeDtypeStruct((LANES, n), F32)],
        compiler_params=pltpu.CompilerParams(dimension_semantics=("parallel",),
                                             vmem_limit_bytes=VMEM_LIMIT),
        name="outproj",
    )(x2d, ohg, osb, w_bf16, g, b, wr, br)


def _moe_kernel(h_ref, gate_ref, w1_ref, w3_ref, w2_ref, g_ref, b_ref, o_ref, acc_ref):
    e = pl.program_id(1)

    @pl.when(e == 0)
    def _():
        acc_ref[...] = jnp.zeros(acc_ref.shape, F32)

    hb = h_ref[...].astype(BF16)
    a = _dot(hb, w1_ref[0])
    hid = a * jax.nn.sigmoid(a) * _dot(hb, w3_ref[0])
    gates = gate_ref[...].T
    lane = lax.broadcasted_iota(jnp.int32, (1, LANES), 1)
    ge = jnp.sum(jnp.where(lane == e, gates, 0.0), axis=-1, keepdims=True)
    acc_ref[...] += ge * _dot(hid.astype(BF16), w2_ref[0])

    @pl.when(e == N_EXPERTS - 1)
    def _():
        o_ref[...] = _layer_norm(ALPHA * h_ref[...] + acc_ref[...], g_ref[...], b_ref[...])


def _moe(h1, gates_t, w1, w3, w2, g, b, tm):
    n = h1.shape[0]
    return pl.pallas_call(
        _moe_kernel,
        grid=(n // tm, N_EXPERTS),
        in_specs=[pl.BlockSpec((tm, D_MODEL), lambda i, e: (i, 0)),
                  pl.BlockSpec((LANES, tm), lambda i, e: (0, i)),
                  pl.BlockSpec((1, D_MODEL, D_EXPERT), lambda i, e: (e, 0, 0)),
                  pl.BlockSpec((1, D_MODEL, D_EXPERT), lambda i, e: (e, 0, 0)),
                  pl.BlockSpec((1, D_EXPERT, D_MODEL), lambda i, e: (e, 0, 0)),
                  pl.BlockSpec((1, D_MODEL), lambda i, e: (0, 0)),
                  pl.BlockSpec((1, D_MODEL), lambda i, e: (0, 0))],
        out_specs=pl.BlockSpec((tm, D_MODEL), lambda i, e: (i, 0)),
        out_shape=jax.ShapeDtypeStruct((n, D_MODEL), F32),
        scratch_shapes=[pltpu.VMEM((tm, D_MODEL), F32)],
        compiler_params=pltpu.CompilerParams(dimension_semantics=("parallel", "arbitrary"),
                                             vmem_limit_bytes=VMEM_LIMIT),
        name="moe",
    )(h1, gates_t, w1, w3, w2, g, b)


def kernel(x, meta_tokens, w_in, hg_lower_bound, hg_norm_g, sb_norm_g, w_out, ln1_g, ln1_b,
           w_router_group, b_router_group, w_router_expert, b_router_expert,
           w_exp_gate, w_exp_up, w_exp_down, ln2_g, ln2_b):
    batch, seq, d = x.shape
    assert d == D_MODEL and seq % HG_CHUNK == 0 and seq % SB_TILE == 0
    assert w_in.shape[0] == DEPTH == 1
    n = batch * seq
    x2d = x.reshape(n, d)

    w_in_b = w_in[0].astype(BF16)
    w_out_b = w_out[0].astype(BF16)
    lb = jnp.cumsum(jax.nn.softmax(hg_lower_bound.astype(F32), axis=0), axis=0)[0].reshape(1, HG_WIDTH)
    wr = jnp.concatenate([w_router_group[0].T,
                          w_router_expert[0].reshape(d, N_EXPERTS).T,
                          jnp.zeros((32 - N_GROUPS - N_EXPERTS, d), F32)], axis=0)
    br = jnp.concatenate([b_router_group[0], b_router_expert[0].reshape(N_EXPERTS),
                          jnp.zeros((32 - N_GROUPS - N_EXPERTS,), F32)]).reshape(32, 1)
    meta_pad = jnp.concatenate([jnp.zeros((META_TILE - N_META, d), F32), meta_tokens.astype(F32)], axis=0)

    hq, hf, hi, hg, sq, sk, sv = _proj(x2d, w_in_b, 512)
    _, mf, mi, _, _, mk, mv = _proj(meta_pad, w_in_b, META_TILE)

    o_hg = _hgrn(hq, hf, hi, hg, mf, mi, lb, hg_norm_g[0].reshape(1, HG_WIDTH), batch, seq)
    o_sb = _stickbreak(sq, sk, sv, mk, mv, sb_norm_g[0].reshape(1, SB_WIDTH), batch, seq)

    h1, gates_t = _outproj(x2d, o_hg, o_sb, w_out_b, ln1_g[0].reshape(1, d), ln1_b[0].reshape(1, d),
                           wr, br, 512)
    out = _moe(h1, gates_t, w_exp_gate[0].astype(BF16), w_exp_up[0].astype(BF16),
               w_exp_down[0].astype(BF16), ln2_g[0].reshape(1, d), ln2_b[0].reshape(1, d), 1024)
    return out.reshape(batch, seq, d)
```

```python
import functools

import jax
import jax.numpy as jnp
from jax import lax
from jax.experimental import pallas as pl
from jax.experimental.pallas import tpu as pltpu

F32 = jnp.float32
BF16 = jnp.bfloat16

D_MODEL = 1024
N_META = 16
HG_WIDTH = 512
HG_HEADS = 4
HG_DK = 128
SB_WIDTH = 512
SB_DH = 64
SB_PAIRS = 4
N_GROUPS = 4
EXPERTS_PER_GROUP = 4
N_EXPERTS = 16
D_EXPERT = 512
DEPTH = 1
ALPHA = (2 * DEPTH) ** 0.25
LN_EPS = 1e-5
RMS_EPS = 1e-6
N_SPLITS = 7
PROJ_DTYPES = (F32, F32, F32, F32, BF16, BF16, BF16)

N_PAIRS = 6
N_CLASSES = N_GROUPS * N_PAIRS
CLS_ROWS = 32
HX_WIDTH = D_MODEL + 128
MOE_TILE = 512
PERM_ROWS = 4096

LANES = 128
HG_CHUNK = 128
SB_TILE = 128
META_TILE = 128
SKIP_LOG = -88.0
VMEM_LIMIT = 56 * 1024 * 1024

NT_DIMS = (((1,), (1,)), ((), ()))


def _dot(a, b):
    return jnp.dot(a, b, preferred_element_type=F32)


def _dot_nt(a, b):
    return lax.dot_general(a, b, NT_DIMS, preferred_element_type=F32)


def _split3(x):
    hi = x.astype(BF16)
    r1 = x - hi.astype(F32)
    mid = r1.astype(BF16)
    lo = (r1 - mid.astype(F32)).astype(BF16)
    return hi, mid, lo


def _split2(x):
    hi = x.astype(BF16)
    return hi, (x - hi.astype(F32)).astype(BF16)


def _proj_kernel(x_ref, w_ref, *out_refs):
    xb = x_ref[...].astype(BF16)
    for j, o_ref in enumerate(out_refs):
        o_ref[...] = _dot(xb, w_ref[:, j * 512:(j + 1) * 512]).astype(o_ref.dtype)


def _proj(x2d, w_bf16, tm):
    n = x2d.shape[0]
    out_shape = [jax.ShapeDtypeStruct((n, 512), dt) for dt in PROJ_DTYPES]
    return pl.pallas_call(
        _proj_kernel,
        grid=(n // tm,),
        in_specs=[pl.BlockSpec((tm, D_MODEL), lambda i: (i, 0)),
                  pl.BlockSpec((D_MODEL, N_SPLITS * 512), lambda i: (0, 0))],
        out_specs=[pl.BlockSpec((tm, 512), lambda i: (i, 0)) for _ in range(N_SPLITS)],
        out_shape=out_shape,
        compiler_params=pltpu.CompilerParams(dimension_semantics=("parallel",),
                                             vmem_limit_bytes=VMEM_LIMIT),
        name="proj",
    )(x2d, w_bf16)


def _hgrn_chunk(zfs, zis, lbs, sts, tri, zqs=None, causal=None):
    n = len(zfs)
    c = zfs[0].shape[0]
    ks = [(1.0 - lbs[h]) * jax.nn.sigmoid(-zfs[h]) for h in range(n)]
    splits = [_split3(jnp.log(1.0 - k)) for k in ks]
    bcs = [_dot(tri, hi) + _dot(tri, mid) + _dot(tri, lo) for hi, mid, lo in splits]
    b_last = [bc[c - 1:c, :] for bc in bcs]
    os = None
    if zqs is not None:
        qs = [zq * jax.nn.sigmoid(zq) for zq in zqs]
        refs = [bc[c // 2 - 1:c // 2, :] for bc in bcs]
        qes = [(qs[h] * jnp.exp(bcs[h] - refs[h])).astype(BF16) for h in range(n)]
        kes = [(ks[h] * jnp.exp(refs[h] - bcs[h])).astype(BF16) for h in range(n)]
        qbs = [(qs[h] * jnp.exp(bcs[h])).astype(BF16) for h in range(n)]
        scs = [jnp.where(causal, _dot_nt(qes[h], kes[h]), 0.0).astype(BF16) for h in range(n)]
        os = [_dot(scs[h], zis[h].astype(BF16)) + _dot_nt(qbs[h], sts[h].astype(BF16)) for h in range(n)]
    kls = [(ks[h] * jnp.exp(b_last[h] - bcs[h])).astype(BF16) for h in range(n)]
    new_sts = [sts[h] * jnp.exp(b_last[h]) + _dot(zis[h].T.astype(BF16), kls[h]) for h in range(n)]
    return new_sts, os


def _hgrn_kernel(q_ref, f_ref, i_ref, g_ref, mf_ref, mi_ref, lb_ref, ng_ref, o_ref, st_ref):
    c = HG_CHUNK
    row = lax.broadcasted_iota(jnp.int32, (c, c), 0)
    col = lax.broadcasted_iota(jnp.int32, (c, c), 1)
    causal = col <= row
    tri = causal.astype(BF16)
    sls = [slice(h * HG_DK, (h + 1) * HG_DK) for h in range(HG_HEADS)]
    lbs = [lb_ref[:, sl] for sl in sls]

    @pl.when(pl.program_id(1) == 0)
    def _():
        zero = jnp.zeros((HG_DK, HG_DK), F32)
        sts, _ = _hgrn_chunk([mf_ref[:, sl] for sl in sls], [mi_ref[:, sl] for sl in sls], lbs,
                             [zero] * HG_HEADS, tri)
        for h in range(HG_HEADS):
            st_ref[h] = sts[h]

    sts, os = _hgrn_chunk([f_ref[:, sl] for sl in sls], [i_ref[:, sl] for sl in sls], lbs,
                          [st_ref[h] for h in range(HG_HEADS)], tri,
                          zqs=[q_ref[:, sl] for sl in sls], causal=causal)
    for h in range(HG_HEADS):
        st_ref[h] = sts[h]
    for h, sl in enumerate(sls):
        ms = jnp.mean(os[h] * os[h], axis=-1, keepdims=True)
        y = os[h] * lax.rsqrt(ms + RMS_EPS) * ng_ref[:, sl] * jax.nn.sigmoid(g_ref[:, sl])
        o_ref[:, sl] = y.astype(o_ref.dtype)


def _hgrn(hq, hf, hi, hg, mf, mi, lb, ng, batch, seq):
    nc = seq // HG_CHUNK
    blk = pl.BlockSpec((HG_CHUNK, HG_WIDTH), lambda b, c: (b * nc + c, 0))
    meta = pl.BlockSpec((META_TILE, HG_WIDTH), lambda b, c: (0, 0))
    vec = pl.BlockSpec((1, HG_WIDTH), lambda b, c: (0, 0))
    return pl.pallas_call(
        _hgrn_kernel,
        grid=(batch, nc),
        in_specs=[blk, blk, blk, blk, meta, meta, vec, vec],
        out_specs=blk,
        out_shape=jax.ShapeDtypeStruct((batch * seq, HG_WIDTH), BF16),
        scratch_shapes=[pltpu.VMEM((HG_HEADS, HG_DK, HG_DK), F32)],
        compiler_params=pltpu.CompilerParams(dimension_semantics=("parallel", "arbitrary"),
                                             vmem_limit_bytes=VMEM_LIMIT),
        name="hgrn2",
    )(hq, hf, hi, hg, mf, mi, lb, ng)


def _sb_tiles(qs, ks, vs, rev, carries, mask):
    n = len(qs)
    zs = [_dot_nt(qs[p], ks[p]) for p in range(n)]
    ls_pos, lks = [], []
    for z in zs:
        sp = jnp.log(1.0 + jnp.exp(-jnp.abs(z)))
        ls_pos.append(jnp.minimum(z, 0.0) - sp)
        lk = jnp.minimum(-z, 0.0) - sp
        lks.append(lk if mask is None else jnp.where(mask, lk, 0.0))
    splits = [_split2(lk) for lk in lks]
    css = [_dot(hi, rev) + _dot(lo, rev) for hi, lo in splits]
    outs = []
    for p in range(n):
        a = jnp.exp(ls_pos[p] + css[p] + carries[p])
        if mask is not None:
            a = jnp.where(mask, a, 0.0)
        outs.append(a.astype(BF16))
    parts = [_dot(outs[p], vs[p]) for p in range(n)]
    new_carries = [carries[p] + jnp.sum(lks[p], axis=-1, keepdims=True) for p in range(n)]
    return parts, new_carries


def _sb_kernel(q_ref, k_ref, v_ref, mk_ref, mv_ref, ng_ref, o_ref, acc_ref, car_ref):
    t = SB_TILE
    qi = pl.program_id(1)
    lane = lax.broadcasted_iota(jnp.int32, (1, LANES), 1)
    lo = lane < SB_DH
    row = lax.broadcasted_iota(jnp.int32, (2 * t, t), 0)
    col = lax.broadcasted_iota(jnp.int32, (2 * t, t), 1)
    rev = (lax.broadcasted_iota(jnp.int32, (t, t), 0)
           > lax.broadcasted_iota(jnp.int32, (t, t), 1)).astype(BF16)
    diag_mask = col < jnp.where(row >= t, row - t, row)
    meta_mask = col >= META_TILE - N_META

    def pair(ref, s, p):
        return ref[pl.ds(s, t), p * LANES:(p + 1) * LANES]

    qs = []
    for p in range(SB_PAIRS):
        q = q_ref[:, p * LANES:(p + 1) * LANES] * (SB_DH ** -0.5)
        qs.append(jnp.concatenate([jnp.where(lo, q, 0.0), jnp.where(lo, 0.0, q)], axis=0).astype(BF16))

    def sweep(k_of, v_of, carries, mask):
        return _sb_tiles(qs, [k_of(p) for p in range(SB_PAIRS)], [v_of(p) for p in range(SB_PAIRS)],
                         rev, carries, mask)

    def alive_of(carries):
        m = carries[0]
        for c in carries[1:]:
            m = jnp.maximum(m, c)
        return jnp.max(m) > SKIP_LOG

    zero = jnp.zeros((2 * t, 1), F32)
    start = pl.multiple_of(qi * t, t)
    accs, carries = sweep(lambda p: pair(k_ref, start, p), lambda p: pair(v_ref, start, p),
                          [zero] * SB_PAIRS, diag_mask)

    def cond(state):
        return jnp.logical_and(state[0] >= 0, state[1])

    def body(state):
        kt, _, accs, carries = state
        s = pl.multiple_of(kt * t, t)
        parts, carries = sweep(lambda p: pair(k_ref, s, p), lambda p: pair(v_ref, s, p), carries, None)
        return kt - 1, alive_of(carries), [a + b for a, b in zip(accs, parts)], carries

    _, alive, accs, carries = lax.while_loop(cond, body, (qi - 1, alive_of(carries), accs, carries))
    for p in range(SB_PAIRS):
        acc_ref[p] = accs[p]
        car_ref[p] = carries[p]

    @pl.when(alive)
    def _():
        parts, _ = sweep(lambda p: mk_ref[:, p * LANES:(p + 1) * LANES],
                         lambda p: mv_ref[:, p * LANES:(p + 1) * LANES],
                         [car_ref[p] for p in range(SB_PAIRS)], meta_mask)
        for p in range(SB_PAIRS):
            acc_ref[p] += parts[p]

    for p in range(SB_PAIRS):
        sl = slice(p * LANES, (p + 1) * LANES)
        o = jnp.where(lo, acc_ref[p, :t, :], acc_ref[p, t:, :])
        sq = o * o
        ms_lo = jnp.sum(jnp.where(lo, sq, 0.0), axis=-1, keepdims=True) * (1.0 / SB_DH)
        ms_hi = jnp.sum(jnp.where(lo, 0.0, sq), axis=-1, keepdims=True) * (1.0 / SB_DH)
        inv = jnp.where(lo, lax.rsqrt(ms_lo + RMS_EPS), lax.rsqrt(ms_hi + RMS_EPS))
        o_ref[:, sl] = (o * inv * ng_ref[:, sl]).astype(o_ref.dtype)


def _stickbreak(sq, sk, sv, mk, mv, ng, batch, seq):
    nq = seq // SB_TILE
    qblk = pl.BlockSpec((SB_TILE, SB_WIDTH), lambda b, i: (b * nq + i, 0))
    kvblk = pl.BlockSpec((seq, SB_WIDTH), lambda b, i: (b, 0))
    mblk = pl.BlockSpec((META_TILE, SB_WIDTH), lambda b, i: (0, 0))
    gblk = pl.BlockSpec((1, SB_WIDTH), lambda b, i: (0, 0))
    return pl.pallas_call(
        _sb_kernel,
        grid=(batch, nq),
        in_specs=[qblk, kvblk, kvblk, mblk, mblk, gblk],
        out_specs=qblk,
        out_shape=jax.ShapeDtypeStruct((batch * seq, SB_WIDTH), BF16),
        scratch_shapes=[pltpu.VMEM((SB_PAIRS, 2 * SB_TILE, LANES), F32),
                        pltpu.VMEM((SB_PAIRS, 2 * SB_TILE, 1), F32)],
        compiler_params=pltpu.CompilerParams(dimension_semantics=("parallel", "arbitrary"),
                                             vmem_limit_bytes=VMEM_LIMIT),
        name="stickbreak",
    )(sq, sk, sv, mk, mv, ng)


def _layer_norm(x, g, b):
    mu = jnp.mean(x, axis=-1, keepdims=True)
    xc = x - mu
    var = jnp.mean(xc * xc, axis=-1, keepdims=True)
    return xc * lax.rsqrt(var + LN_EPS) * g + b


def _argmax_first(rows):
    best = rows[0]
    idx = jnp.zeros(best.shape, jnp.int32)
    for j in range(1, len(rows)):
        better = rows[j] > best
        idx = jnp.where(better, j, idx)
        best = jnp.where(better, rows[j], best)
    return best, idx


def _outproj_kernel(x_ref, ohg_ref, osb_ref, w_ref, g_ref, b_ref, wr_hi_ref, wr_mid_ref, wr_lo_ref, br_ref,
                    hx_ref, cls_ref, rank_ref, cnt_ref, run_ref):
    tm = x_ref.shape[0]

    @pl.when(pl.program_id(0) == 0)
    def _():
        run_ref[...] = jnp.zeros(run_ref.shape, F32)

    y = _dot(ohg_ref[...], w_ref[:HG_WIDTH, :]) + _dot(osb_ref[...], w_ref[HG_WIDTH:, :])
    h1 = _layer_norm(ALPHA * x_ref[...] + y, g_ref[...], b_ref[...])
    hx_ref[:, :D_MODEL] = h1

    h_hi, h_mid, h_lo = _split3(h1)
    w_hi, w_mid, w_lo = wr_hi_ref[...], wr_mid_ref[...], wr_lo_ref[...]
    lg = (_dot(h_hi, w_hi) + _dot(h_mid, w_hi) + _dot(h_hi, w_mid)
          + _dot(h_lo, w_hi) + _dot(h_hi, w_lo) + _dot(h_mid, w_mid))
    lt = lg.T + br_ref[...]

    lgs = [lt[g:g + 1, :] for g in range(N_GROUPS)]
    best, grp = _argmax_first(lgs)
    denom = jnp.exp(lgs[0] - best)
    for g in range(1, N_GROUPS):
        denom = denom + jnp.exp(lgs[g] - best)
    p_grp = 1.0 / denom

    li = []
    for e in range(EXPERTS_PER_GROUP):
        acc = jnp.zeros_like(best)
        for g in range(N_GROUPS):
            r = N_GROUPS + g * EXPERTS_PER_GROUP + e
            acc = acc + jnp.where(grp == g, lt[r:r + 1, :], 0.0)
        li.append(acc)
    v1, i1 = _argmax_first(li)
    neg = jnp.full_like(v1, -jnp.inf)
    v2, i2 = _argmax_first([jnp.where(i1 == e, neg, li[e]) for e in range(EXPERTS_PER_GROUP)])
    ex = jnp.exp(v2 - v1)
    w1 = 1.0 / (1.0 + ex)
    w2 = ex / (1.0 + ex)

    first = i1 < i2
    e_lo = jnp.where(first, i1, i2)
    e_hi = jnp.where(first, i2, i1)
    pair = jnp.where(e_lo == 0, e_hi - 1, jnp.where(e_lo == 1, e_hi + 1, 5))
    cls = grp * N_PAIRS + pair
    g_lo = p_grp * jnp.where(first, w1, w2)
    g_hi = p_grp * jnp.where(first, w2, w1)
    sub = lax.broadcasted_iota(jnp.int32, (LANES, tm), 0)
    hx_ref[:, D_MODEL:] = jnp.where(sub == 0, g_lo, jnp.where(sub == 1, g_hi, 0.0)).T

    onehot = lax.broadcasted_iota(jnp.int32, (CLS_ROWS, tm), 0) == cls
    oh = onehot.astype(BF16)
    earlier = (lax.broadcasted_iota(jnp.int32, (tm, tm), 0)
               < lax.broadcasted_iota(jnp.int32, (tm, tm), 1)).astype(BF16)
    run = run_ref[...]
    rank = jnp.sum(jnp.where(onehot, _dot(oh, earlier) + run, 0.0), axis=0, keepdims=True)
    cls_ref[...] = cls
    rank_ref[...] = rank.astype(jnp.int32)
    run = run + jnp.sum(oh.astype(F32), axis=1, keepdims=True)
    run_ref[...] = run
    cnt_ref[...] = jnp.broadcast_to(run, cnt_ref.shape)


def _outproj(x2d, ohg, osb, w_bf16, g, b, wr_split, br, tm):
    n = x2d.shape[0]
    const = lambda shape: pl.BlockSpec(shape, lambda i: (0, 0))
    return pl.pallas_call(
        _outproj_kernel,
        grid=(n // tm,),
        in_specs=[pl.BlockSpec((tm, D_MODEL), lambda i: (i, 0)),
                  pl.BlockSpec((tm, HG_WIDTH), lambda i: (i, 0)),
                  pl.BlockSpec((tm, SB_WIDTH), lambda i: (i, 0)),
                  const((D_MODEL, D_MODEL)), const((1, D_MODEL)), const((1, D_MODEL)),
                  const((D_MODEL, LANES)), const((D_MODEL, LANES)), const((D_MODEL, LANES)),
                  const((LANES, 1))],
        out_specs=[pl.BlockSpec((tm, HX_WIDTH), lambda i: (i, 0)),
                   pl.BlockSpec((1, tm), lambda i: (0, i)),
                   pl.BlockSpec((1, tm), lambda i: (0, i)),
                   const((CLS_ROWS, LANES))],
        out_shape=[jax.ShapeDtypeStruct((n, HX_WIDTH), F32),
                   jax.ShapeDtypeStruct((1, n), jnp.int32),
                   jax.ShapeDtypeStruct((1, n), jnp.int32),
                   jax.ShapeDtypeStruct((CLS_ROWS, LANES), F32)],
        scratch_shapes=[pltpu.VMEM((CLS_ROWS, 1), F32)],
        compiler_params=pltpu.CompilerParams(dimension_semantics=("arbitrary",),
                                             vmem_limit_bytes=VMEM_LIMIT),
        name="outproj",
    )(x2d, ohg, osb, w_bf16, g, b, *wr_split, br)


def _permute_kernel(idx_ref, src_ref, *rest, rows, scatter):
    dst_ref, sem = rest[-2], rest[-1]
    base = pl.program_id(0) * rows

    def row_copy(r):
        j = idx_ref[base + r]
        s, d = (base + r, j) if scatter else (j, base + r)
        return pltpu.make_async_copy(src_ref.at[pl.ds(s, 1)], dst_ref.at[pl.ds(d, 1)], sem)

    def issue(r, carry):
        row_copy(r).start()
        return carry

    lax.fori_loop(0, rows, issue, 0, unroll=8)
    pltpu.make_async_copy(src_ref.at[pl.ds(0, rows)], dst_ref.at[pl.ds(0, rows)], sem).wait()


def _permute_rows(idx, src, dst_rows, scatter, init=None):
    n = idx.shape[0]
    width = src.shape[1]
    any_spec = pl.BlockSpec(memory_space=pl.ANY)
    operands = [idx, src] + ([] if init is None else [init])
    return pl.pallas_call(
        functools.partial(_permute_kernel, rows=PERM_ROWS, scatter=scatter),
        grid_spec=pltpu.PrefetchScalarGridSpec(
            num_scalar_prefetch=1,
            grid=(n // PERM_ROWS,),
            in_specs=[any_spec] * (len(operands) - 1),
            out_specs=any_spec,
            scratch_shapes=[pltpu.SemaphoreType.DMA(())]),
        out_shape=jax.ShapeDtypeStruct((dst_rows, width), src.dtype),
        input_output_aliases={} if init is None else {2: 0},
        compiler_params=pltpu.CompilerParams(dimension_semantics=("arbitrary",)),
        name="scatter_rows" if scatter else "gather_rows",
    )(*operands)


def _moe_kernel(ea_ref, eb_ref, valid_ref, x_ref, w1a_ref, w3a_ref, w2a_ref, w1b_ref, w3b_ref, w2b_ref,
                g_ref, b_ref, o_ref):
    i = pl.program_id(0)

    @pl.when(valid_ref[i] == 1)
    def _():
        h1 = x_ref[:, :D_MODEL]
        hb = h1.astype(BF16)

        def hidden(w1_ref, w3_ref, gate):
            a = _dot(hb, w1_ref[0])
            return (gate * (a * jax.nn.sigmoid(a)) * _dot(hb, w3_ref[0])).astype(BF16)

        y = (_dot(hidden(w1a_ref, w3a_ref, x_ref[:, D_MODEL:D_MODEL + 1]), w2a_ref[0])
             + _dot(hidden(w1b_ref, w3b_ref, x_ref[:, D_MODEL + 1:D_MODEL + 2]), w2b_ref[0]))
        o_ref[...] = _layer_norm(ALPHA * h1 + y, g_ref[...], b_ref[...])

    @pl.when(valid_ref[i] == 0)
    def _():
        o_ref[...] = jnp.zeros(o_ref.shape, F32)


def _moe(tile_ea, tile_eb, tile_valid, xs, w1, w3, w2, g, b):
    nt = tile_ea.shape[0]
    wa = lambda shape: pl.BlockSpec(shape, lambda i, ea, eb, v: (ea[i], 0, 0))
    wb = lambda shape: pl.BlockSpec(shape, lambda i, ea, eb, v: (eb[i], 0, 0))
    up, down = (1, D_MODEL, D_EXPERT), (1, D_EXPERT, D_MODEL)
    return pl.pallas_call(
        _moe_kernel,
        grid_spec=pltpu.PrefetchScalarGridSpec(
            num_scalar_prefetch=3,
            grid=(nt,),
            in_specs=[pl.BlockSpec((MOE_TILE, HX_WIDTH), lambda i, ea, eb, v: (i, 0)),
                      wa(up), wa(up), wa(down), wb(up), wb(up), wb(down),
                      pl.BlockSpec((1, D_MODEL), lambda i, ea, eb, v: (0, 0)),
                      pl.BlockSpec((1, D_MODEL), lambda i, ea, eb, v: (0, 0))],
            out_specs=pl.BlockSpec((MOE_TILE, D_MODEL), lambda i, ea, eb, v: (i, 0))),
        out_shape=jax.ShapeDtypeStruct((nt * MOE_TILE, D_MODEL), F32),
        compiler_params=pltpu.CompilerParams(dimension_semantics=("arbitrary",),
                                             vmem_limit_bytes=VMEM_LIMIT),
        name="moe",
    )(tile_ea, tile_eb, tile_valid, xs, w1, w3, w2, w1, w3, w2, g, b)


def _route(cls, rank, counts, n):
    cnt = counts[:N_CLASSES, 0].astype(jnp.int32)
    ntile = (cnt + MOE_TILE - 1) // MOE_TILE
    tile_end = jnp.cumsum(ntile)
    seg_off = (tile_end - ntile) * MOE_TILE
    pos = seg_off[cls.reshape(n)] + rank.reshape(n)
    tiles = jnp.arange(n // MOE_TILE + N_CLASSES, dtype=jnp.int32)
    tcls = jnp.minimum(jnp.searchsorted(tile_end, tiles, side="right"), N_CLASSES - 1).astype(jnp.int32)
    valid = (tiles < tile_end[-1]).astype(jnp.int32)
    pair_lo = jnp.array([0, 0, 0, 1, 1, 2], jnp.int32)
    pair_hi = jnp.array([1, 2, 3, 2, 3, 3], jnp.int32)
    grp, pair = tcls // N_PAIRS, tcls % N_PAIRS
    return pos, grp * EXPERTS_PER_GROUP + pair_lo[pair], grp * EXPERTS_PER_GROUP + pair_hi[pair], valid


def kernel(x, meta_tokens, w_in, hg_lower_bound, hg_norm_g, sb_norm_g, w_out, ln1_g, ln1_b,
           w_router_group, b_router_group, w_router_expert, b_router_expert,
           w_exp_gate, w_exp_up, w_exp_down, ln2_g, ln2_b):
    batch, seq, d = x.shape
    assert d == D_MODEL and seq % HG_CHUNK == 0 and seq % SB_TILE == 0
    assert w_in.shape[0] == DEPTH == 1
    n = batch * seq
    x2d = x.reshape(n, d)

    w_in_b = w_in[0].astype(BF16)
    w_out_b = w_out[0].astype(BF16)
    lb = jnp.cumsum(jax.nn.softmax(hg_lower_bound.astype(F32), axis=0), axis=0)[0].reshape(1, HG_WIDTH)
    n_logits = N_GROUPS + N_EXPERTS
    wr = jnp.concatenate([w_router_group[0], w_router_expert[0].reshape(d, N_EXPERTS),
                          jnp.zeros((d, LANES - n_logits), F32)], axis=1)
    wr_hi = wr.astype(BF16)
    wr_mid = (wr - wr_hi.astype(F32)).astype(BF16)
    wr_lo = (wr - wr_hi.astype(F32) - wr_mid.astype(F32)).astype(BF16)
    br = jnp.concatenate([b_router_group[0], b_router_expert[0].reshape(N_EXPERTS),
                          jnp.zeros((LANES - n_logits,), F32)]).reshape(LANES, 1)
    meta_pad = jnp.concatenate([jnp.zeros((META_TILE - N_META, d), F32), meta_tokens.astype(F32)], axis=0)

    hq, hf, hi, hg, sq, sk, sv = _proj(x2d, w_in_b, 512)
    _, mf, mi, _, _, mk, mv = _proj(meta_pad, w_in_b, META_TILE)

    o_hg = _hgrn(hq, hf, hi, hg, mf, mi, lb, hg_norm_g[0].reshape(1, HG_WIDTH), batch, seq)
    o_sb = _stickbreak(sq, sk, sv, mk, mv, sb_norm_g[0].reshape(1, SB_WIDTH), batch, seq)

    hx, cls, rank, counts = _outproj(x2d, o_hg, o_sb, w_out_b, ln1_g[0].reshape(1, d), ln1_b[0].reshape(1, d),
                                     (wr_hi, wr_mid, wr_lo), br, 512)
    pos, tile_ea, tile_eb, tile_valid = _route(cls, rank, counts, n)
    sorted_rows = tile_ea.shape[0] * MOE_TILE
    xs = _permute_rows(pos, hx, sorted_rows, scatter=True, init=jnp.zeros((sorted_rows, HX_WIDTH), F32))
    ys = _moe(tile_ea, tile_eb, tile_valid, xs, w_exp_gate[0].astype(BF16), w_exp_up[0].astype(BF16),
              w_exp_down[0].astype(BF16), ln2_g[0].reshape(1, d), ln2_b[0].reshape(1, d))
    out = _permute_rows(pos, ys, n, scatter=False)
    return out.reshape(batch, seq, d)
```

```python
import functools

import jax
import jax.numpy as jnp
from jax import lax
from jax.experimental import pallas as pl
from jax.experimental.pallas import tpu as pltpu

F32 = jnp.float32
BF16 = jnp.bfloat16

D_MODEL = 1024
N_META = 16
HG_WIDTH = 512
HG_HEADS = 4
HG_DK = 128
SB_WIDTH = 512
SB_DH = 64
SB_PAIRS = 4
N_GROUPS = 4
EXPERTS_PER_GROUP = 4
N_EXPERTS = 16
D_EXPERT = 512
DEPTH = 1
ALPHA = (2 * DEPTH) ** 0.25
LN_EPS = 1e-5
RMS_EPS = 1e-6
N_SPLITS = 7
PROJ_DTYPES = (F32, F32, F32, F32, BF16, BF16, BF16)

N_PAIRS = 6
N_CLASSES = N_GROUPS * N_PAIRS
CLS_ROWS = 32
H_ROWS = D_MODEL // 128
SLAB = 2 * H_ROWS
MOE_TILE = 512
PERM_TOKENS = 512
ZERO_TOKENS = 64

LANES = 128
HG_CHUNK = 128
SB_TILE = 128
META_TILE = 128
SKIP_LOG = -88.0
VMEM_LIMIT = 56 * 1024 * 1024

NT_DIMS = (((1,), (1,)), ((), ()))


def _dot(a, b):
    return jnp.dot(a, b, preferred_element_type=F32)


def _dot_nt(a, b):
    return lax.dot_general(a, b, NT_DIMS, preferred_element_type=F32)


def _split3(x):
    hi = x.astype(BF16)
    r1 = x - hi.astype(F32)
    mid = r1.astype(BF16)
    lo = (r1 - mid.astype(F32)).astype(BF16)
    return hi, mid, lo


def _split2(x):
    hi = x.astype(BF16)
    return hi, (x - hi.astype(F32)).astype(BF16)


def _proj_kernel(x_ref, w_ref, *out_refs):
    xb = x_ref[...].astype(BF16)
    for j, o_ref in enumerate(out_refs):
        o_ref[...] = _dot(xb, w_ref[:, j * 512:(j + 1) * 512]).astype(o_ref.dtype)


def _proj(x2d, w_bf16, tm):
    n = x2d.shape[0]
    out_shape = [jax.ShapeDtypeStruct((n, 512), dt) for dt in PROJ_DTYPES]
    return pl.pallas_call(
        _proj_kernel,
        grid=(n // tm,),
        in_specs=[pl.BlockSpec((tm, D_MODEL), lambda i: (i, 0)),
                  pl.BlockSpec((D_MODEL, N_SPLITS * 512), lambda i: (0, 0))],
        out_specs=[pl.BlockSpec((tm, 512), lambda i: (i, 0)) for _ in range(N_SPLITS)],
        out_shape=out_shape,
        compiler_params=pltpu.CompilerParams(dimension_semantics=("parallel",),
                                             vmem_limit_bytes=VMEM_LIMIT),
        name="proj",
    )(x2d, w_bf16)


def _hgrn_chunk(zfs, zis, lbs, sts, tri, zqs=None, causal=None):
    n = len(zfs)
    c = zfs[0].shape[0]
    ks = [(1.0 - lbs[h]) * jax.nn.sigmoid(-zfs[h]) for h in range(n)]
    splits = [_split3(jnp.log(1.0 - k)) for k in ks]
    bcs = [_dot(tri, hi) + _dot(tri, mid) + _dot(tri, lo) for hi, mid, lo in splits]
    b_last = [bc[c - 1:c, :] for bc in bcs]
    os = None
    if zqs is not None:
        qs = [zq * jax.nn.sigmoid(zq) for zq in zqs]
        refs = [bc[c // 2 - 1:c // 2, :] for bc in bcs]
        qes = [(qs[h] * jnp.exp(bcs[h] - refs[h])).astype(BF16) for h in range(n)]
        kes = [(ks[h] * jnp.exp(refs[h] - bcs[h])).astype(BF16) for h in range(n)]
        qbs = [(qs[h] * jnp.exp(bcs[h])).astype(BF16) for h in range(n)]
        scs = [jnp.where(causal, _dot_nt(qes[h], kes[h]), 0.0).astype(BF16) for h in range(n)]
        os = [_dot(scs[h], zis[h].astype(BF16)) + _dot_nt(qbs[h], sts[h].astype(BF16)) for h in range(n)]
    kls = [(ks[h] * jnp.exp(b_last[h] - bcs[h])).astype(BF16) for h in range(n)]
    new_sts = [sts[h] * jnp.exp(b_last[h]) + _dot(zis[h].T.astype(BF16), kls[h]) for h in range(n)]
    return new_sts, os


def _hgrn_kernel(q_ref, f_ref, i_ref, g_ref, mf_ref, mi_ref, lb_ref, ng_ref, o_ref, st_ref):
    c = HG_CHUNK
    row = lax.broadcasted_iota(jnp.int32, (c, c), 0)
    col = lax.broadcasted_iota(jnp.int32, (c, c), 1)
    causal = col <= row
    tri = causal.astype(BF16)
    sls = [slice(h * HG_DK, (h + 1) * HG_DK) for h in range(HG_HEADS)]
    lbs = [lb_ref[:, sl] for sl in sls]

    @pl.when(pl.program_id(1) == 0)
    def _():
        zero = jnp.zeros((HG_DK, HG_DK), F32)
        sts, _ = _hgrn_chunk([mf_ref[:, sl] for sl in sls], [mi_ref[:, sl] for sl in sls], lbs,
                             [zero] * HG_HEADS, tri)
        for h in range(HG_HEADS):
            st_ref[h] = sts[h]

    sts, os = _hgrn_chunk([f_ref[:, sl] for sl in sls], [i_ref[:, sl] for sl in sls], lbs,
                          [st_ref[h] for h in range(HG_HEADS)], tri,
                          zqs=[q_ref[:, sl] for sl in sls], causal=causal)
    for h in range(HG_HEADS):
        st_ref[h] = sts[h]
    for h, sl in enumerate(sls):
        ms = jnp.mean(os[h] * os[h], axis=-1, keepdims=True)
        y = os[h] * lax.rsqrt(ms + RMS_EPS) * ng_ref[:, sl] * jax.nn.sigmoid(g_ref[:, sl])
        o_ref[:, sl] = y.astype(o_ref.dtype)


def _hgrn(hq, hf, hi, hg, mf, mi, lb, ng, batch, seq):
    nc = seq // HG_CHUNK
    blk = pl.BlockSpec((HG_CHUNK, HG_WIDTH), lambda b, c: (b * nc + c, 0))
    meta = pl.BlockSpec((META_TILE, HG_WIDTH), lambda b, c: (0, 0))
    vec = pl.BlockSpec((1, HG_WIDTH), lambda b, c: (0, 0))
    return pl.pallas_call(
        _hgrn_kernel,
        grid=(batch, nc),
        in_specs=[blk, blk, blk, blk, meta, meta, vec, vec],
        out_specs=blk,
        out_shape=jax.ShapeDtypeStruct((batch * seq, HG_WIDTH), BF16),
        scratch_shapes=[pltpu.VMEM((HG_HEADS, HG_DK, HG_DK), F32)],
        compiler_params=pltpu.CompilerParams(dimension_semantics=("parallel", "arbitrary"),
                                             vmem_limit_bytes=VMEM_LIMIT),
        name="hgrn2",
    )(hq, hf, hi, hg, mf, mi, lb, ng)


def _sb_tiles(qs, ks, vs, rev, carries, mask):
    n = len(qs)
    zs = [_dot_nt(qs[p], ks[p]) for p in range(n)]
    ls_pos, lks = [], []
    for z in zs:
        sp = jnp.log(1.0 + jnp.exp(-jnp.abs(z)))
        ls_pos.append(jnp.minimum(z, 0.0) - sp)
        lk = jnp.minimum(-z, 0.0) - sp
        lks.append(lk if mask is None else jnp.where(mask, lk, 0.0))
    splits = [_split2(lk) for lk in lks]
    css = [_dot(hi, rev) + _dot(lo, rev) for hi, lo in splits]
    outs = []
    for p in range(n):
        a = jnp.exp(ls_pos[p] + css[p] + carries[p])
        if mask is not None:
            a = jnp.where(mask, a, 0.0)
        outs.append(a.astype(BF16))
    parts = [_dot(outs[p], vs[p]) for p in range(n)]
    new_carries = [carries[p] + jnp.sum(lks[p], axis=-1, keepdims=True) for p in range(n)]
    return parts, new_carries


def _sb_kernel(q_ref, k_ref, v_ref, mk_ref, mv_ref, ng_ref, o_ref, acc_ref, car_ref):
    t = SB_TILE
    qi = pl.program_id(1)
    lane = lax.broadcasted_iota(jnp.int32, (1, LANES), 1)
    lo = lane < SB_DH
    row = lax.broadcasted_iota(jnp.int32, (2 * t, t), 0)
    col = lax.broadcasted_iota(jnp.int32, (2 * t, t), 1)
    rev = (lax.broadcasted_iota(jnp.int32, (t, t), 0)
           > lax.broadcasted_iota(jnp.int32, (t, t), 1)).astype(BF16)
    diag_mask = col < jnp.where(row >= t, row - t, row)
    meta_mask = col >= META_TILE - N_META

    def pair(ref, s, p):
        return ref[pl.ds(s, t), p * LANES:(p + 1) * LANES]

    qs = []
    for p in range(SB_PAIRS):
        q = q_ref[:, p * LANES:(p + 1) * LANES] * (SB_DH ** -0.5)
        qs.append(jnp.concatenate([jnp.where(lo, q, 0.0), jnp.where(lo, 0.0, q)], axis=0).astype(BF16))

    def sweep(k_of, v_of, carries, mask):
        return _sb_tiles(qs, [k_of(p) for p in range(SB_PAIRS)], [v_of(p) for p in range(SB_PAIRS)],
                         rev, carries, mask)

    def alive_of(carries):
        m = carries[0]
        for c in carries[1:]:
            m = jnp.maximum(m, c)
        return jnp.max(m) > SKIP_LOG

    zero = jnp.zeros((2 * t, 1), F32)
    start = pl.multiple_of(qi * t, t)
    accs, carries = sweep(lambda p: pair(k_ref, start, p), lambda p: pair(v_ref, start, p),
                          [zero] * SB_PAIRS, diag_mask)

    def cond(state):
        return jnp.logical_and(state[0] >= 0, state[1])

    def body(state):
        kt, _, accs, carries = state
        s = pl.multiple_of(kt * t, t)
        parts, carries = sweep(lambda p: pair(k_ref, s, p), lambda p: pair(v_ref, s, p), carries, None)
        return kt - 1, alive_of(carries), [a + b for a, b in zip(accs, parts)], carries

    _, alive, accs, carries = lax.while_loop(cond, body, (qi - 1, alive_of(carries), accs, carries))
    for p in range(SB_PAIRS):
        acc_ref[p] = accs[p]
        car_ref[p] = carries[p]

    @pl.when(alive)
    def _():
        parts, _ = sweep(lambda p: mk_ref[:, p * LANES:(p + 1) * LANES],
                         lambda p: mv_ref[:, p * LANES:(p + 1) * LANES],
                         [car_ref[p] for p in range(SB_PAIRS)], meta_mask)
        for p in range(SB_PAIRS):
            acc_ref[p] += parts[p]

    for p in range(SB_PAIRS):
        sl = slice(p * LANES, (p + 1) * LANES)
        o = jnp.where(lo, acc_ref[p, :t, :], acc_ref[p, t:, :])
        sq = o * o
        ms_lo = jnp.sum(jnp.where(lo, sq, 0.0), axis=-1, keepdims=True) * (1.0 / SB_DH)
        ms_hi = jnp.sum(jnp.where(lo, 0.0, sq), axis=-1, keepdims=True) * (1.0 / SB_DH)
        inv = jnp.where(lo, lax.rsqrt(ms_lo + RMS_EPS), lax.rsqrt(ms_hi + RMS_EPS))
        o_ref[:, sl] = (o * inv * ng_ref[:, sl]).astype(o_ref.dtype)


def _stickbreak(sq, sk, sv, mk, mv, ng, batch, seq):
    nq = seq // SB_TILE
    qblk = pl.BlockSpec((SB_TILE, SB_WIDTH), lambda b, i: (b * nq + i, 0))
    kvblk = pl.BlockSpec((seq, SB_WIDTH), lambda b, i: (b, 0))
    mblk = pl.BlockSpec((META_TILE, SB_WIDTH), lambda b, i: (0, 0))
    gblk = pl.BlockSpec((1, SB_WIDTH), lambda b, i: (0, 0))
    return pl.pallas_call(
        _sb_kernel,
        grid=(batch, nq),
        in_specs=[qblk, kvblk, kvblk, mblk, mblk, gblk],
        out_specs=qblk,
        out_shape=jax.ShapeDtypeStruct((batch * seq, SB_WIDTH), BF16),
        scratch_shapes=[pltpu.VMEM((SB_PAIRS, 2 * SB_TILE, LANES), F32),
                        pltpu.VMEM((SB_PAIRS, 2 * SB_TILE, 1), F32)],
        compiler_params=pltpu.CompilerParams(dimension_semantics=("parallel", "arbitrary"),
                                             vmem_limit_bytes=VMEM_LIMIT),
        name="stickbreak",
    )(sq, sk, sv, mk, mv, ng)


def _layer_norm(x, g, b):
    mu = jnp.mean(x, axis=-1, keepdims=True)
    xc = x - mu
    var = jnp.mean(xc * xc, axis=-1, keepdims=True)
    return xc * lax.rsqrt(var + LN_EPS) * g + b


def _argmax_first(rows):
    best = rows[0]
    idx = jnp.zeros(best.shape, jnp.int32)
    for j in range(1, len(rows)):
        better = rows[j] > best
        idx = jnp.where(better, j, idx)
        best = jnp.where(better, rows[j], best)
    return best, idx


def _outproj_kernel(x_ref, ohg_ref, osb_ref, w_ref, g_ref, b_ref, wr_hi_ref, wr_mid_ref, wr_lo_ref, br_ref,
                    hx_ref, cls_ref, rank_ref, cnt_ref, run_ref):
    tm = x_ref.shape[0]

    @pl.when(pl.program_id(0) == 0)
    def _():
        run_ref[...] = jnp.zeros(run_ref.shape, F32)

    y = _dot(ohg_ref[...], w_ref[:HG_WIDTH, :]) + _dot(osb_ref[...], w_ref[HG_WIDTH:, :])
    h1 = _layer_norm(ALPHA * x_ref[...] + y, g_ref[...], b_ref[...])
    for s in range(H_ROWS):
        hx_ref[pl.ds(s, tm, stride=SLAB), :] = h1[:, s * LANES:(s + 1) * LANES]

    h_hi, h_mid, h_lo = _split3(h1)
    w_hi, w_mid, w_lo = wr_hi_ref[...], wr_mid_ref[...], wr_lo_ref[...]
    lg = (_dot(h_hi, w_hi) + _dot(h_mid, w_hi) + _dot(h_hi, w_mid)
          + _dot(h_lo, w_hi) + _dot(h_hi, w_lo) + _dot(h_mid, w_mid))
    lt = lg.T + br_ref[...]

    lgs = [lt[g:g + 1, :] for g in range(N_GROUPS)]
    best, grp = _argmax_first(lgs)
    denom = jnp.exp(lgs[0] - best)
    for g in range(1, N_GROUPS):
        denom = denom + jnp.exp(lgs[g] - best)
    p_grp = 1.0 / denom

    li = []
    for e in range(EXPERTS_PER_GROUP):
        acc = jnp.zeros_like(best)
        for g in range(N_GROUPS):
            r = N_GROUPS + g * EXPERTS_PER_GROUP + e
            acc = acc + jnp.where(grp == g, lt[r:r + 1, :], 0.0)
        li.append(acc)
    v1, i1 = _argmax_first(li)
    neg = jnp.full_like(v1, -jnp.inf)
    v2, i2 = _argmax_first([jnp.where(i1 == e, neg, li[e]) for e in range(EXPERTS_PER_GROUP)])
    ex = jnp.exp(v2 - v1)
    w1 = 1.0 / (1.0 + ex)
    w2 = ex / (1.0 + ex)

    first = i1 < i2
    e_lo = jnp.where(first, i1, i2)
    e_hi = jnp.where(first, i2, i1)
    pair = jnp.where(e_lo == 0, e_hi - 1, jnp.where(e_lo == 1, e_hi + 1, 5))
    cls = grp * N_PAIRS + pair
    g_lo = p_grp * jnp.where(first, w1, w2)
    g_hi = p_grp * jnp.where(first, w2, w1)
    sub = lax.broadcasted_iota(jnp.int32, (LANES, tm), 0)
    gates = jnp.where(sub == 0, g_lo, jnp.where(sub == 1, g_hi, 0.0)).T
    for s in range(H_ROWS, SLAB):
        hx_ref[pl.ds(s, tm, stride=SLAB), :] = gates

    onehot = lax.broadcasted_iota(jnp.int32, (CLS_ROWS, tm), 0) == cls
    oh = onehot.astype(BF16)
    earlier = (lax.broadcasted_iota(jnp.int32, (tm, tm), 0)
               < lax.broadcasted_iota(jnp.int32, (tm, tm), 1)).astype(BF16)
    run = run_ref[...]
    rank = jnp.sum(jnp.where(onehot, _dot(oh, earlier) + run, 0.0), axis=0, keepdims=True)
    cls_ref[...] = cls
    rank_ref[...] = rank.astype(jnp.int32)
    run = run + jnp.sum(oh.astype(F32), axis=1, keepdims=True)
    run_ref[...] = run
    cnt_ref[...] = jnp.broadcast_to(run, cnt_ref.shape)


def _outproj(x2d, ohg, osb, w_bf16, g, b, wr_split, br, tm):
    n = x2d.shape[0]
    const = lambda shape: pl.BlockSpec(shape, lambda i: (0, 0))
    return pl.pallas_call(
        _outproj_kernel,
        grid=(n // tm,),
        in_specs=[pl.BlockSpec((tm, D_MODEL), lambda i: (i, 0)),
                  pl.BlockSpec((tm, HG_WIDTH), lambda i: (i, 0)),
                  pl.BlockSpec((tm, SB_WIDTH), lambda i: (i, 0)),
                  const((D_MODEL, D_MODEL)), const((1, D_MODEL)), const((1, D_MODEL)),
                  const((D_MODEL, LANES)), const((D_MODEL, LANES)), const((D_MODEL, LANES)),
                  const((LANES, 1))],
        out_specs=[pl.BlockSpec((tm * SLAB, LANES), lambda i: (i, 0)),
                   pl.BlockSpec((1, tm), lambda i: (0, i)),
                   pl.BlockSpec((1, tm), lambda i: (0, i)),
                   const((CLS_ROWS, LANES))],
        out_shape=[jax.ShapeDtypeStruct((n * SLAB, LANES), F32),
                   jax.ShapeDtypeStruct((1, n), jnp.int32),
                   jax.ShapeDtypeStruct((1, n), jnp.int32),
                   jax.ShapeDtypeStruct((CLS_ROWS, LANES), F32)],
        scratch_shapes=[pltpu.VMEM((CLS_ROWS, 1), F32)],
        compiler_params=pltpu.CompilerParams(dimension_semantics=("arbitrary",),
                                             vmem_limit_bytes=VMEM_LIMIT),
        name="outproj",
    )(x2d, ohg, osb, w_bf16, g, b, *wr_split, br)


def _slab(ref, token, rows):
    return ref.at[pl.ds(pl.multiple_of(token * rows, rows), rows)]


def _scatter_kernel(pos_ref, zfrom_ref, zto_ref, src_ref, dst_ref, zero_ref, sem, zsem):
    base = pl.program_id(0) * PERM_TOKENS

    @pl.when(pl.program_id(0) == 0)
    def _():
        zero_ref[...] = jnp.zeros(zero_ref.shape, F32)

        def fill(k):
            return pltpu.make_async_copy(zero_ref, _slab(dst_ref, k, ZERO_TOKENS * SLAB), zsem)

        def each_chunk(do):
            def per_range(c, carry):
                def per_chunk(k, carry):
                    do(fill(k))
                    return carry
                return lax.fori_loop(zfrom_ref[c], zto_ref[c], per_chunk, carry)
            lax.fori_loop(0, zfrom_ref.shape[0], per_range, 0)

        each_chunk(lambda copy: copy.start())
        each_chunk(lambda copy: copy.wait())

    def issue(r, carry):
        pltpu.make_async_copy(_slab(src_ref, r, SLAB), _slab(dst_ref, pos_ref[base + r], SLAB), sem).start()
        return carry

    lax.fori_loop(0, PERM_TOKENS, issue, 0, unroll=8)
    pltpu.make_async_copy(src_ref, dst_ref.at[pl.ds(0, PERM_TOKENS * SLAB)], sem).wait()


def _scatter_tokens(pos, zfrom, zto, hx, sorted_tokens):
    n = pos.shape[0]
    return pl.pallas_call(
        _scatter_kernel,
        grid_spec=pltpu.PrefetchScalarGridSpec(
            num_scalar_prefetch=3,
            grid=(n // PERM_TOKENS,),
            in_specs=[pl.BlockSpec((PERM_TOKENS * SLAB, LANES), lambda i, pos, zf, zt: (i, 0))],
            out_specs=pl.BlockSpec(memory_space=pl.ANY),
            scratch_shapes=[pltpu.VMEM((ZERO_TOKENS * SLAB, LANES), F32),
                            pltpu.SemaphoreType.DMA(()), pltpu.SemaphoreType.DMA(())]),
        out_shape=jax.ShapeDtypeStruct((sorted_tokens * SLAB, LANES), F32),
        compiler_params=pltpu.CompilerParams(dimension_semantics=("arbitrary",),
                                             vmem_limit_bytes=VMEM_LIMIT),
        name="scatter_tokens",
    )(pos, zfrom, zto, hx)


def _gather_kernel(pos_ref, src_ref, o_ref, buf_ref, sem):
    base = pl.program_id(0) * PERM_TOKENS

    def issue(r, carry):
        pltpu.make_async_copy(_slab(src_ref, pos_ref[base + r], H_ROWS), _slab(buf_ref, r, H_ROWS), sem).start()
        return carry

    lax.fori_loop(0, PERM_TOKENS, issue, 0, unroll=8)
    pltpu.make_async_copy(src_ref.at[pl.ds(0, PERM_TOKENS * H_ROWS)], buf_ref, sem).wait()
    for s in range(H_ROWS):
        o_ref[:, s * LANES:(s + 1) * LANES] = buf_ref[pl.ds(s, PERM_TOKENS, stride=H_ROWS), :]


def _gather_tokens(pos, ys):
    n = pos.shape[0]
    return pl.pallas_call(
        _gather_kernel,
        grid_spec=pltpu.PrefetchScalarGridSpec(
            num_scalar_prefetch=1,
            grid=(n // PERM_TOKENS,),
            in_specs=[pl.BlockSpec(memory_space=pl.ANY)],
            out_specs=pl.BlockSpec((PERM_TOKENS, D_MODEL), lambda i, pos: (i, 0)),
            scratch_shapes=[pltpu.VMEM((PERM_TOKENS * H_ROWS, LANES), F32), pltpu.SemaphoreType.DMA(())]),
        out_shape=jax.ShapeDtypeStruct((n, D_MODEL), F32),
        compiler_params=pltpu.CompilerParams(dimension_semantics=("arbitrary",),
                                             vmem_limit_bytes=VMEM_LIMIT),
        name="gather_tokens",
    )(pos, ys)


def _moe_kernel(ea_ref, eb_ref, rows_ref, x_ref, w1a_ref, w3a_ref, w2a_ref, w1b_ref, w3b_ref, w2b_ref,
                g_ref, b_ref, o_ref):
    i = pl.program_id(0)
    t = MOE_TILE

    @pl.when(rows_ref[i] > 0)
    def _():
        h1 = jnp.concatenate([x_ref[pl.ds(s, t, stride=SLAB), :] for s in range(H_ROWS)], axis=1)
        gates = x_ref[pl.ds(H_ROWS, t, stride=SLAB), :]
        hb = h1.astype(BF16)

        def hidden(w1_ref, w3_ref, gate):
            a = _dot(hb, w1_ref[0])
            return (gate * (a * jax.nn.sigmoid(a)) * _dot(hb, w3_ref[0])).astype(BF16)

        y = (_dot(hidden(w1a_ref, w3a_ref, gates[:, 0:1]), w2a_ref[0])
             + _dot(hidden(w1b_ref, w3b_ref, gates[:, 1:2]), w2b_ref[0]))
        out = _layer_norm(ALPHA * h1 + y, g_ref[...], b_ref[...])
        for s in range(H_ROWS):
            o_ref[pl.ds(s, t, stride=H_ROWS), :] = out[:, s * LANES:(s + 1) * LANES]

    @pl.when(rows_ref[i] == 0)
    def _():
        o_ref[...] = jnp.zeros(o_ref.shape, F32)


def _moe(tile_ea, tile_eb, tile_rows, xs, w1, w3, w2, g, b):
    nt = tile_ea.shape[0]
    wa = lambda shape: pl.BlockSpec(shape, lambda i, ea, eb, r: (ea[i], 0, 0))
    wb = lambda shape: pl.BlockSpec(shape, lambda i, ea, eb, r: (eb[i], 0, 0))
    up, down = (1, D_MODEL, D_EXPERT), (1, D_EXPERT, D_MODEL)
    return pl.pallas_call(
        _moe_kernel,
        grid_spec=pltpu.PrefetchScalarGridSpec(
            num_scalar_prefetch=3,
            grid=(nt,),
            in_specs=[pl.BlockSpec((MOE_TILE * SLAB, LANES), lambda i, ea, eb, r: (i, 0)),
                      wa(up), wa(up), wa(down), wb(up), wb(up), wb(down),
                      pl.BlockSpec((1, D_MODEL), lambda i, ea, eb, r: (0, 0)),
                      pl.BlockSpec((1, D_MODEL), lambda i, ea, eb, r: (0, 0))],
            out_specs=pl.BlockSpec((MOE_TILE * H_ROWS, LANES), lambda i, ea, eb, r: (i, 0))),
        out_shape=jax.ShapeDtypeStruct((nt * MOE_TILE * H_ROWS, LANES), F32),
        compiler_params=pltpu.CompilerParams(dimension_semantics=("arbitrary",),
                                             vmem_limit_bytes=VMEM_LIMIT),
        name="moe",
    )(tile_ea, tile_eb, tile_rows, xs, w1, w3, w2, w1, w3, w2, g, b)


def _route(cls, rank, counts, n):
    cnt = counts[:N_CLASSES, 0].astype(jnp.int32)
    ntile = (cnt + MOE_TILE - 1) // MOE_TILE
    tile_end = jnp.cumsum(ntile)
    tile_start = tile_end - ntile
    pos = (tile_start * MOE_TILE)[cls.reshape(n)] + rank.reshape(n)
    tiles = jnp.arange(n // MOE_TILE + N_CLASSES, dtype=jnp.int32)
    tcls = jnp.minimum(jnp.sum((tiles[:, None] >= tile_end[None, :]).astype(jnp.int32), axis=1), N_CLASSES - 1)
    rows = jnp.clip(cnt[tcls] - (tiles - tile_start[tcls]) * MOE_TILE, 0, MOE_TILE)
    rows = jnp.where(tiles < tile_end[-1], rows, 0)
    pair_lo = jnp.array([0, 0, 0, 1, 1, 2], jnp.int32)
    pair_hi = jnp.array([1, 2, 3, 2, 3, 3], jnp.int32)
    grp, pair = tcls // N_PAIRS, tcls % N_PAIRS
    per_tile = MOE_TILE // ZERO_TOKENS
    zfrom = jnp.concatenate([(tile_start * MOE_TILE + cnt) // ZERO_TOKENS, tile_end[-1:] * per_tile])
    zto = jnp.concatenate([tile_end * per_tile, jnp.full((1,), tiles.shape[0] * per_tile, jnp.int32)])
    return (pos, grp * EXPERTS_PER_GROUP + pair_lo[pair], grp * EXPERTS_PER_GROUP + pair_hi[pair], rows,
            zfrom.astype(jnp.int32), zto.astype(jnp.int32))


def kernel(x, meta_tokens, w_in, hg_lower_bound, hg_norm_g, sb_norm_g, w_out, ln1_g, ln1_b,
           w_router_group, b_router_group, w_router_expert, b_router_expert,
           w_exp_gate, w_exp_up, w_exp_down, ln2_g, ln2_b):
    batch, seq, d = x.shape
    assert d == D_MODEL and seq % HG_CHUNK == 0 and seq % SB_TILE == 0
    assert w_in.shape[0] == DEPTH == 1
    n = batch * seq
    assert n % PERM_TOKENS == 0 and n % MOE_TILE == 0
    x2d = x.reshape(n, d)

    w_in_b = w_in[0].astype(BF16)
    w_out_b = w_out[0].astype(BF16)
    lb = jnp.cumsum(jax.nn.softmax(hg_lower_bound.astype(F32), axis=0), axis=0)[0].reshape(1, HG_WIDTH)
    n_logits = N_GROUPS + N_EXPERTS
    wr = jnp.concatenate([w_router_group[0], w_router_expert[0].reshape(d, N_EXPERTS),
                          jnp.zeros((d, LANES - n_logits), F32)], axis=1)
    wr_hi = wr.astype(BF16)
    wr_mid = (wr - wr_hi.astype(F32)).astype(BF16)
    wr_lo = (wr - wr_hi.astype(F32) - wr_mid.astype(F32)).astype(BF16)
    br = jnp.concatenate([b_router_group[0], b_router_expert[0].reshape(N_EXPERTS),
                          jnp.zeros((LANES - n_logits,), F32)]).reshape(LANES, 1)
    meta_pad = jnp.concatenate([jnp.zeros((META_TILE - N_META, d), F32), meta_tokens.astype(F32)], axis=0)

    hq, hf, hi, hg, sq, sk, sv = _proj(x2d, w_in_b, 512)
    _, mf, mi, _, _, mk, mv = _proj(meta_pad, w_in_b, META_TILE)

    o_hg = _hgrn(hq, hf, hi, hg, mf, mi, lb, hg_norm_g[0].reshape(1, HG_WIDTH), batch, seq)
    o_sb = _stickbreak(sq, sk, sv, mk, mv, sb_norm_g[0].reshape(1, SB_WIDTH), batch, seq)

    hx, cls, rank, counts = _outproj(x2d, o_hg, o_sb, w_out_b, ln1_g[0].reshape(1, d), ln1_b[0].reshape(1, d),
                                     (wr_hi, wr_mid, wr_lo), br, 512)
    pos, tile_ea, tile_eb, tile_rows, zfrom, zto = _route(cls, rank, counts, n)
    xs = _scatter_tokens(pos, zfrom, zto, hx, tile_ea.shape[0] * MOE_TILE)
    ys = _moe(tile_ea, tile_eb, tile_rows, xs, w_exp_gate[0].astype(BF16), w_exp_up[0].astype(BF16),
              w_exp_down[0].astype(BF16), ln2_g[0].reshape(1, d), ln2_b[0].reshape(1, d))
    return _gather_tokens(pos, ys).reshape(batch, seq, d)
```

```python
import functools

import jax
import jax.numpy as jnp
from jax import lax
from jax.experimental import pallas as pl
from jax.experimental.pallas import tpu as pltpu

F32 = jnp.float32
BF16 = jnp.bfloat16

D_MODEL = 1024
N_META = 16
HG_WIDTH = 512
HG_HEADS = 4
HG_DK = 128
SB_WIDTH = 512
SB_DH = 64
SB_PAIRS = 4
N_GROUPS = 4
EXPERTS_PER_GROUP = 4
N_EXPERTS = 16
D_EXPERT = 512
DEPTH = 1
ALPHA = (2 * DEPTH) ** 0.25
LN_EPS = 1e-5
RMS_EPS = 1e-6
N_SPLITS = 7
PROJ_DTYPES = (F32, F32, F32, F32, BF16, BF16, BF16)

N_PAIRS = 6
N_CLASSES = N_GROUPS * N_PAIRS
CLS_ROWS = 32
H_ROWS = D_MODEL // 128
SLAB = 2 * H_ROWS
MOE_TILE = 512
PERM_TOKENS = 1024
ZERO_TOKENS = 64
ISSUE_UNROLL = 8

LANES = 128
HG_CHUNK = 128
HG_STEP_CHUNKS = 2
SB_TILE = 128
META_TILE = 128
SKIP_LOG = -88.0
VMEM_LIMIT = 56 * 1024 * 1024

NT_DIMS = (((1,), (1,)), ((), ()))


def _dot(a, b):
    return jnp.dot(a, b, preferred_element_type=F32)


def _dot_nt(a, b):
    return lax.dot_general(a, b, NT_DIMS, preferred_element_type=F32)


def _split3(x):
    hi = x.astype(BF16)
    r1 = x - hi.astype(F32)
    mid = r1.astype(BF16)
    lo = (r1 - mid.astype(F32)).astype(BF16)
    return hi, mid, lo


def _split2(x):
    hi = x.astype(BF16)
    return hi, (x - hi.astype(F32)).astype(BF16)


def _proj_kernel(x_ref, w_ref, *out_refs):
    xb = x_ref[...].astype(BF16)
    for j, o_ref in enumerate(out_refs):
        o_ref[...] = _dot(xb, w_ref[:, j * 512:(j + 1) * 512]).astype(o_ref.dtype)


def _proj(x2d, w_bf16, tm):
    n = x2d.shape[0]
    out_shape = [jax.ShapeDtypeStruct((n, 512), dt) for dt in PROJ_DTYPES]
    return pl.pallas_call(
        _proj_kernel,
        grid=(n // tm,),
        in_specs=[pl.BlockSpec((tm, D_MODEL), lambda i: (i, 0)),
                  pl.BlockSpec((D_MODEL, N_SPLITS * 512), lambda i: (0, 0))],
        out_specs=[pl.BlockSpec((tm, 512), lambda i: (i, 0)) for _ in range(N_SPLITS)],
        out_shape=out_shape,
        compiler_params=pltpu.CompilerParams(dimension_semantics=("parallel",),
                                             vmem_limit_bytes=VMEM_LIMIT),
        name="proj",
    )(x2d, w_bf16)


def _hgrn_chunk(zfs, zis, lbs, sts, tri, zqs=None, causal=None):
    n = len(zfs)
    c = zfs[0].shape[0]
    ks = [(1.0 - lbs[h]) * jax.nn.sigmoid(-zfs[h]) for h in range(n)]
    splits = [_split3(jnp.log(1.0 - k)) for k in ks]
    bcs = [_dot(tri, hi) + _dot(tri, mid) + _dot(tri, lo) for hi, mid, lo in splits]
    b_last = [bc[c - 1:c, :] for bc in bcs]
    os = None
    if zqs is not None:
        qs = [zq * jax.nn.sigmoid(zq) for zq in zqs]
        refs = [bc[c // 2 - 1:c // 2, :] for bc in bcs]
        qes = [(qs[h] * jnp.exp(bcs[h] - refs[h])).astype(BF16) for h in range(n)]
        kes = [(ks[h] * jnp.exp(refs[h] - bcs[h])).astype(BF16) for h in range(n)]
        qbs = [(qs[h] * jnp.exp(bcs[h])).astype(BF16) for h in range(n)]
        scs = [jnp.where(causal, _dot_nt(qes[h], kes[h]), 0.0).astype(BF16) for h in range(n)]
        os = [_dot(scs[h], zis[h].astype(BF16)) + _dot_nt(qbs[h], sts[h].astype(BF16)) for h in range(n)]
    kls = [(ks[h] * jnp.exp(b_last[h] - bcs[h])).astype(BF16) for h in range(n)]
    new_sts = [sts[h] * jnp.exp(b_last[h]) + _dot(zis[h].T.astype(BF16), kls[h]) for h in range(n)]
    return new_sts, os


def _hgrn_kernel(q_ref, f_ref, i_ref, g_ref, mf_ref, mi_ref, lb_ref, ng_ref, o_ref, st_ref):
    c = HG_CHUNK
    row = lax.broadcasted_iota(jnp.int32, (c, c), 0)
    col = lax.broadcasted_iota(jnp.int32, (c, c), 1)
    causal = col <= row
    tri = causal.astype(BF16)
    sls = [slice(h * HG_DK, (h + 1) * HG_DK) for h in range(HG_HEADS)]
    lbs = [lb_ref[:, sl] for sl in sls]

    @pl.when(pl.program_id(1) == 0)
    def _():
        zero = jnp.zeros((HG_DK, HG_DK), F32)
        sts, _ = _hgrn_chunk([mf_ref[:, sl] for sl in sls], [mi_ref[:, sl] for sl in sls], lbs,
                             [zero] * HG_HEADS, tri)
        for h in range(HG_HEADS):
            st_ref[h] = sts[h]

    sts = [st_ref[h] for h in range(HG_HEADS)]
    for j in range(HG_STEP_CHUNKS):
        rows = slice(j * c, (j + 1) * c)
        sts, os = _hgrn_chunk([f_ref[rows, sl] for sl in sls], [i_ref[rows, sl] for sl in sls], lbs, sts, tri,
                              zqs=[q_ref[rows, sl] for sl in sls], causal=causal)
        for h, sl in enumerate(sls):
            ms = jnp.mean(os[h] * os[h], axis=-1, keepdims=True)
            y = os[h] * lax.rsqrt(ms + RMS_EPS) * ng_ref[:, sl] * jax.nn.sigmoid(g_ref[rows, sl])
            o_ref[rows, sl] = y.astype(o_ref.dtype)
    for h in range(HG_HEADS):
        st_ref[h] = sts[h]


def _hgrn(hq, hf, hi, hg, mf, mi, lb, ng, batch, seq):
    step_rows = HG_CHUNK * HG_STEP_CHUNKS
    nc = seq // step_rows
    blk = pl.BlockSpec((step_rows, HG_WIDTH), lambda b, c: (b * nc + c, 0))
    meta = pl.BlockSpec((META_TILE, HG_WIDTH), lambda b, c: (0, 0))
    vec = pl.BlockSpec((1, HG_WIDTH), lambda b, c: (0, 0))
    return pl.pallas_call(
        _hgrn_kernel,
        grid=(batch, nc),
        in_specs=[blk, blk, blk, blk, meta, meta, vec, vec],
        out_specs=blk,
        out_shape=jax.ShapeDtypeStruct((batch * seq, HG_WIDTH), BF16),
        scratch_shapes=[pltpu.VMEM((HG_HEADS, HG_DK, HG_DK), F32)],
        compiler_params=pltpu.CompilerParams(dimension_semantics=("parallel", "arbitrary"),
                                             vmem_limit_bytes=VMEM_LIMIT),
        name="hgrn2",
    )(hq, hf, hi, hg, mf, mi, lb, ng)


def _sb_tiles(qs, ks, vs, rev, carries, mask):
    n = len(qs)
    zs = [_dot_nt(qs[p], ks[p]) for p in range(n)]
    ls_pos, lks = [], []
    for z in zs:
        sp = jnp.log(1.0 + jnp.exp(-jnp.abs(z)))
        ls = jnp.minimum(z, 0.0) - sp
        ls_pos.append(ls)
        lk = ls - z
        lks.append(lk if mask is None else jnp.where(mask, lk, 0.0))
    splits = [_split2(lk) for lk in lks]
    css = [_dot(hi, rev) + _dot(lo, rev) for hi, lo in splits]
    outs = []
    for p in range(n):
        a = jnp.exp(ls_pos[p] + css[p] + carries[p])
        if mask is not None:
            a = jnp.where(mask, a, 0.0)
        outs.append(a.astype(BF16))
    parts = [_dot(outs[p], vs[p]) for p in range(n)]
    new_carries = [carries[p] + jnp.sum(lks[p], axis=-1, keepdims=True) for p in range(n)]
    return parts, new_carries


def _sb_kernel(q_ref, k_ref, v_ref, mk_ref, mv_ref, ng_ref, o_ref, acc_ref, car_ref):
    t = SB_TILE
    qi = pl.program_id(1)
    lane = lax.broadcasted_iota(jnp.int32, (1, LANES), 1)
    lo = lane < SB_DH
    row = lax.broadcasted_iota(jnp.int32, (2 * t, t), 0)
    col = lax.broadcasted_iota(jnp.int32, (2 * t, t), 1)
    rev = (lax.broadcasted_iota(jnp.int32, (t, t), 0)
           > lax.broadcasted_iota(jnp.int32, (t, t), 1)).astype(BF16)
    diag_mask = col < jnp.where(row >= t, row - t, row)
    meta_mask = col >= META_TILE - N_META

    def pair(ref, s, p):
        return ref[pl.ds(s, t), p * LANES:(p + 1) * LANES]

    qs = []
    for p in range(SB_PAIRS):
        q = q_ref[:, p * LANES:(p + 1) * LANES] * (SB_DH ** -0.5)
        qs.append(jnp.concatenate([jnp.where(lo, q, 0.0), jnp.where(lo, 0.0, q)], axis=0).astype(BF16))

    def sweep(k_of, v_of, carries, mask):
        return _sb_tiles(qs, [k_of(p) for p in range(SB_PAIRS)], [v_of(p) for p in range(SB_PAIRS)],
                         rev, carries, mask)

    def alive_of(carries):
        m = carries[0]
        for c in carries[1:]:
            m = jnp.maximum(m, c)
        return jnp.max(m) > SKIP_LOG

    zero = jnp.zeros((2 * t, 1), F32)
    start = pl.multiple_of(qi * t, t)
    accs, carries = sweep(lambda p: pair(k_ref, start, p), lambda p: pair(v_ref, start, p),
                          [zero] * SB_PAIRS, diag_mask)

    def cond(state):
        return jnp.logical_and(state[0] >= 0, state[1])

    def body(state):
        kt, _, accs, carries = state
        s = pl.multiple_of(kt * t, t)
        parts, carries = sweep(lambda p: pair(k_ref, s, p), lambda p: pair(v_ref, s, p), carries, None)
        return kt - 1, alive_of(carries), [a + b for a, b in zip(accs, parts)], carries

    _, alive, accs, carries = lax.while_loop(cond, body, (qi - 1, alive_of(carries), accs, carries))
    for p in range(SB_PAIRS):
        acc_ref[p] = accs[p]
        car_ref[p] = carries[p]

    @pl.when(alive)
    def _():
        parts, _ = sweep(lambda p: mk_ref[:, p * LANES:(p + 1) * LANES],
                         lambda p: mv_ref[:, p * LANES:(p + 1) * LANES],
                         [car_ref[p] for p in range(SB_PAIRS)], meta_mask)
        for p in range(SB_PAIRS):
            acc_ref[p] += parts[p]

    for p in range(SB_PAIRS):
        sl = slice(p * LANES, (p + 1) * LANES)
        o = jnp.where(lo, acc_ref[p, :t, :], acc_ref[p, t:, :])
        sq = o * o
        ms_lo = jnp.sum(jnp.where(lo, sq, 0.0), axis=-1, keepdims=True) * (1.0 / SB_DH)
        ms_hi = jnp.sum(jnp.where(lo, 0.0, sq), axis=-1, keepdims=True) * (1.0 / SB_DH)
        inv = jnp.where(lo, lax.rsqrt(ms_lo + RMS_EPS), lax.rsqrt(ms_hi + RMS_EPS))
        o_ref[:, sl] = (o * inv * ng_ref[:, sl]).astype(o_ref.dtype)


def _stickbreak(sq, sk, sv, mk, mv, ng, batch, seq):
    nq = seq // SB_TILE
    qblk = pl.BlockSpec((SB_TILE, SB_WIDTH), lambda b, i: (b * nq + i, 0))
    kvblk = pl.BlockSpec((seq, SB_WIDTH), lambda b, i: (b, 0))
    mblk = pl.BlockSpec((META_TILE, SB_WIDTH), lambda b, i: (0, 0))
    gblk = pl.BlockSpec((1, SB_WIDTH), lambda b, i: (0, 0))
    return pl.pallas_call(
        _sb_kernel,
        grid=(batch, nq),
        in_specs=[qblk, kvblk, kvblk, mblk, mblk, gblk],
        out_specs=qblk,
        out_shape=jax.ShapeDtypeStruct((batch * seq, SB_WIDTH), BF16),
        scratch_shapes=[pltpu.VMEM((SB_PAIRS, 2 * SB_TILE, LANES), F32),
                        pltpu.VMEM((SB_PAIRS, 2 * SB_TILE, 1), F32)],
        compiler_params=pltpu.CompilerParams(dimension_semantics=("parallel", "arbitrary"),
                                             vmem_limit_bytes=VMEM_LIMIT),
        name="stickbreak",
    )(sq, sk, sv, mk, mv, ng)


def _layer_norm(x, g, b):
    mu = jnp.mean(x, axis=-1, keepdims=True)
    xc = x - mu
    var = jnp.mean(xc * xc, axis=-1, keepdims=True)
    return xc * lax.rsqrt(var + LN_EPS) * g + b


def _argmax_first(rows):
    best = rows[0]
    idx = jnp.zeros(best.shape, jnp.int32)
    for j in range(1, len(rows)):
        better = rows[j] > best
        idx = jnp.where(better, j, idx)
        best = jnp.where(better, rows[j], best)
    return best, idx


def _outproj_kernel(x_ref, ohg_ref, osb_ref, w_ref, g_ref, b_ref, wr_hi_ref, wr_mid_ref, br_ref,
                    hx_ref, cls_ref, rank_ref, cnt_ref, run_ref):
    tm = x_ref.shape[0]

    @pl.when(pl.program_id(0) == 0)
    def _():
        run_ref[...] = jnp.zeros(run_ref.shape, F32)

    y = _dot(jnp.concatenate([ohg_ref[...], osb_ref[...]], axis=1), w_ref[...])
    h1 = _layer_norm(ALPHA * x_ref[...] + y, g_ref[...], b_ref[...])
    for s in range(H_ROWS):
        hx_ref[pl.ds(s, tm, stride=SLAB), :] = h1[:, s * LANES:(s + 1) * LANES]

    h_hi, h_mid = _split2(h1)
    w_hi = wr_hi_ref[...]
    lg = _dot(h_hi, w_hi) + _dot(h_mid, w_hi) + _dot(h_hi, wr_mid_ref[...])
    lt = lg.T + br_ref[...]

    lgs = [lt[g:g + 1, :] for g in range(N_GROUPS)]
    best, grp = _argmax_first(lgs)
    denom = jnp.exp(lgs[0] - best)
    for g in range(1, N_GROUPS):
        denom = denom + jnp.exp(lgs[g] - best)
    p_grp = 1.0 / denom

    li = []
    for e in range(EXPERTS_PER_GROUP):
        acc = jnp.zeros_like(best)
        for g in range(N_GROUPS):
            r = N_GROUPS + g * EXPERTS_PER_GROUP + e
            acc = acc + jnp.where(grp == g, lt[r:r + 1, :], 0.0)
        li.append(acc)
    v1, i1 = _argmax_first(li)
    neg = jnp.full_like(v1, -jnp.inf)
    v2, i2 = _argmax_first([jnp.where(i1 == e, neg, li[e]) for e in range(EXPERTS_PER_GROUP)])
    ex = jnp.exp(v2 - v1)
    w1 = 1.0 / (1.0 + ex)
    w2 = ex / (1.0 + ex)

    first = i1 < i2
    e_lo = jnp.where(first, i1, i2)
    e_hi = jnp.where(first, i2, i1)
    pair = jnp.where(e_lo == 0, e_hi - 1, jnp.where(e_lo == 1, e_hi + 1, 5))
    cls = grp * N_PAIRS + pair
    g_lo = p_grp * jnp.where(first, w1, w2)
    g_hi = p_grp * jnp.where(first, w2, w1)
    sub = lax.broadcasted_iota(jnp.int32, (LANES, tm), 0)
    gates = jnp.where(sub == 0, g_lo, jnp.where(sub == 1, g_hi, 0.0)).T
    for s in range(H_ROWS, SLAB):
        hx_ref[pl.ds(s, tm, stride=SLAB), :] = gates

    onehot = lax.broadcasted_iota(jnp.int32, (CLS_ROWS, tm), 0) == cls
    oh = onehot.astype(BF16)
    earlier = (lax.broadcasted_iota(jnp.int32, (tm, tm), 0)
               < lax.broadcasted_iota(jnp.int32, (tm, tm), 1)).astype(BF16)
    run = run_ref[...]
    rank = jnp.sum(jnp.where(onehot, _dot(oh, earlier) + run, 0.0), axis=0, keepdims=True)
    cls_ref[...] = cls
    rank_ref[...] = rank.astype(jnp.int32)
    run = run + jnp.sum(oh.astype(F32), axis=1, keepdims=True)
    run_ref[...] = run
    cnt_ref[...] = jnp.broadcast_to(run, cnt_ref.shape)


def _outproj(x2d, ohg, osb, w_bf16, g, b, wr_split, br, tm):
    n = x2d.shape[0]
    const = lambda shape: pl.BlockSpec(shape, lambda i: (0, 0))
    return pl.pallas_call(
        _outproj_kernel,
        grid=(n // tm,),
        in_specs=[pl.BlockSpec((tm, D_MODEL), lambda i: (i, 0)),
                  pl.BlockSpec((tm, HG_WIDTH), lambda i: (i, 0)),
                  pl.BlockSpec((tm, SB_WIDTH), lambda i: (i, 0)),
                  const((D_MODEL, D_MODEL)), const((1, D_MODEL)), const((1, D_MODEL)),
                  const((D_MODEL, LANES)), const((D_MODEL, LANES)), const((LANES, 1))],
        out_specs=[pl.BlockSpec((tm * SLAB, LANES), lambda i: (i, 0)),
                   pl.BlockSpec((1, tm), lambda i: (0, i)),
                   pl.BlockSpec((1, tm), lambda i: (0, i)),
                   const((CLS_ROWS, LANES))],
        out_shape=[jax.ShapeDtypeStruct((n * SLAB, LANES), F32),
                   jax.ShapeDtypeStruct((1, n), jnp.int32),
                   jax.ShapeDtypeStruct((1, n), jnp.int32),
                   jax.ShapeDtypeStruct((CLS_ROWS, LANES), F32)],
        scratch_shapes=[pltpu.VMEM((CLS_ROWS, 1), F32)],
        compiler_params=pltpu.CompilerParams(dimension_semantics=("arbitrary",),
                                             vmem_limit_bytes=VMEM_LIMIT),
        name="outproj",
    )(x2d, ohg, osb, w_bf16, g, b, *wr_split, br)


def _slab(ref, token, rows):
    return ref.at[pl.ds(pl.multiple_of(token * rows, rows), rows)]


def _scatter_kernel(pos_ref, zfrom_ref, zto_ref, src_ref, dst_ref, zero_ref, sem, zsem):
    base = pl.program_id(0) * PERM_TOKENS

    @pl.when(pl.program_id(0) == 0)
    def _():
        zero_ref[...] = jnp.zeros(zero_ref.shape, F32)

        def fill(k):
            return pltpu.make_async_copy(zero_ref, _slab(dst_ref, k, ZERO_TOKENS * SLAB), zsem)

        def each_chunk(do):
            def per_range(c, carry):
                def per_chunk(k, carry):
                    do(fill(k))
                    return carry
                return lax.fori_loop(zfrom_ref[c], zto_ref[c], per_chunk, carry)
            lax.fori_loop(0, zfrom_ref.shape[0], per_range, 0)

        each_chunk(lambda copy: copy.start())
        each_chunk(lambda copy: copy.wait())

    def issue(g, carry):
        for u in range(ISSUE_UNROLL):
            r = g * ISSUE_UNROLL + u
            pltpu.make_async_copy(_slab(src_ref, r, SLAB), _slab(dst_ref, pos_ref[base + r], SLAB),
                                  sem).start(priority=u % 2)
        return carry

    lax.fori_loop(0, PERM_TOKENS // ISSUE_UNROLL, issue, 0)
    pltpu.make_async_copy(src_ref, dst_ref.at[pl.ds(0, PERM_TOKENS * SLAB)], sem).wait()


def _scatter_tokens(pos, zfrom, zto, hx, sorted_tokens):
    n = pos.shape[0]
    return pl.pallas_call(
        _scatter_kernel,
        grid_spec=pltpu.PrefetchScalarGridSpec(
            num_scalar_prefetch=3,
            grid=(n // PERM_TOKENS,),
            in_specs=[pl.BlockSpec((PERM_TOKENS * SLAB, LANES), lambda i, pos, zf, zt: (i, 0))],
            out_specs=pl.BlockSpec(memory_space=pl.ANY),
            scratch_shapes=[pltpu.VMEM((ZERO_TOKENS * SLAB, LANES), F32),
                            pltpu.SemaphoreType.DMA(()), pltpu.SemaphoreType.DMA(())]),
        out_shape=jax.ShapeDtypeStruct((sorted_tokens * SLAB, LANES), F32),
        compiler_params=pltpu.CompilerParams(dimension_semantics=("arbitrary",),
                                             vmem_limit_bytes=VMEM_LIMIT),
        name="scatter_tokens",
    )(pos, zfrom, zto, hx)


def _gather_kernel(pos_ref, src_ref, o_ref, buf_ref, sem):
    base = pl.program_id(0) * PERM_TOKENS

    def issue(g, carry):
        for u in range(ISSUE_UNROLL):
            r = g * ISSUE_UNROLL + u
            pltpu.make_async_copy(_slab(src_ref, pos_ref[base + r], H_ROWS), _slab(buf_ref, r, H_ROWS),
                                  sem).start(priority=u % 2)
        return carry

    lax.fori_loop(0, PERM_TOKENS // ISSUE_UNROLL, issue, 0)
    pltpu.make_async_copy(src_ref.at[pl.ds(0, PERM_TOKENS * H_ROWS)], buf_ref, sem).wait()
    for s in range(H_ROWS):
        o_ref[:, s * LANES:(s + 1) * LANES] = buf_ref[pl.ds(s, PERM_TOKENS, stride=H_ROWS), :]


def _gather_tokens(pos, ys):
    n = pos.shape[0]
    return pl.pallas_call(
        _gather_kernel,
        grid_spec=pltpu.PrefetchScalarGridSpec(
            num_scalar_prefetch=1,
            grid=(n // PERM_TOKENS,),
            in_specs=[pl.BlockSpec(memory_space=pl.ANY)],
            out_specs=pl.BlockSpec((PERM_TOKENS, D_MODEL), lambda i, pos: (i, 0)),
            scratch_shapes=[pltpu.VMEM((PERM_TOKENS * H_ROWS, LANES), F32), pltpu.SemaphoreType.DMA(())]),
        out_shape=jax.ShapeDtypeStruct((n, D_MODEL), F32),
        compiler_params=pltpu.CompilerParams(dimension_semantics=("arbitrary",),
                                             vmem_limit_bytes=VMEM_LIMIT),
        name="gather_tokens",
    )(pos, ys)


def _moe_kernel(ea_ref, eb_ref, rows_ref, x_ref, w1a_ref, w3a_ref, w2a_ref, w1b_ref, w3b_ref, w2b_ref,
                g_ref, b_ref, o_ref):
    i = pl.program_id(0)
    t = MOE_TILE

    @pl.when(rows_ref[i] > 0)
    def _():
        h1 = jnp.concatenate([x_ref[pl.ds(s, t, stride=SLAB), :] for s in range(H_ROWS)], axis=1)
        gates = x_ref[pl.ds(H_ROWS, t, stride=SLAB), :]
        hb = h1.astype(BF16)

        def hidden(w1_ref, w3_ref, gate):
            a = _dot(hb, w1_ref[0])
            return (gate * (a * jax.nn.sigmoid(a)) * _dot(hb, w3_ref[0])).astype(BF16)

        y = (_dot(hidden(w1a_ref, w3a_ref, gates[:, 0:1]), w2a_ref[0])
             + _dot(hidden(w1b_ref, w3b_ref, gates[:, 1:2]), w2b_ref[0]))
        out = _layer_norm(ALPHA * h1 + y, g_ref[...], b_ref[...])
        for s in range(H_ROWS):
            o_ref[pl.ds(s, t, stride=H_ROWS), :] = out[:, s * LANES:(s + 1) * LANES]

    @pl.when(rows_ref[i] == 0)
    def _():
        o_ref[...] = jnp.zeros(o_ref.shape, F32)


def _moe(tile_ea, tile_eb, tile_rows, xs, w1, w3, w2, g, b):
    nt = tile_ea.shape[0]
    wa = lambda shape: pl.BlockSpec(shape, lambda i, ea, eb, r: (ea[i], 0, 0))
    wb = lambda shape: pl.BlockSpec(shape, lambda i, ea, eb, r: (eb[i], 0, 0))
    up, down = (1, D_MODEL, D_EXPERT), (1, D_EXPERT, D_MODEL)
    return pl.pallas_call(
        _moe_kernel,
        grid_spec=pltpu.PrefetchScalarGridSpec(
            num_scalar_prefetch=3,
            grid=(nt,),
            in_specs=[pl.BlockSpec((MOE_TILE * SLAB, LANES), lambda i, ea, eb, r: (i, 0)),
                      wa(up), wa(up), wa(down), wb(up), wb(up), wb(down),
                      pl.BlockSpec((1, D_MODEL), lambda i, ea, eb, r: (0, 0)),
                      pl.BlockSpec((1, D_MODEL), lambda i, ea, eb, r: (0, 0))],
            out_specs=pl.BlockSpec((MOE_TILE * H_ROWS, LANES), lambda i, ea, eb, r: (i, 0))),
        out_shape=jax.ShapeDtypeStruct((nt * MOE_TILE * H_ROWS, LANES), F32),
        compiler_params=pltpu.CompilerParams(dimension_semantics=("arbitrary",),
                                             vmem_limit_bytes=VMEM_LIMIT),
        name="moe",
    )(tile_ea, tile_eb, tile_rows, xs, w1, w3, w2, w1, w3, w2, g, b)


def _route(cls, rank, counts, n):
    cnt = counts[:N_CLASSES, 0].astype(jnp.int32)
    ntile = (cnt + MOE_TILE - 1) // MOE_TILE
    tile_end = jnp.cumsum(ntile)
    tile_start = tile_end - ntile
    pos = (tile_start * MOE_TILE)[cls.reshape(n)] + rank.reshape(n)
    tiles = jnp.arange(n // MOE_TILE + N_CLASSES, dtype=jnp.int32)
    tcls = jnp.minimum(jnp.sum((tiles[:, None] >= tile_end[None, :]).astype(jnp.int32), axis=1), N_CLASSES - 1)
    rows = jnp.clip(cnt[tcls] - (tiles - tile_start[tcls]) * MOE_TILE, 0, MOE_TILE)
    rows = jnp.where(tiles < tile_end[-1], rows, 0)
    pair_lo = jnp.array([0, 0, 0, 1, 1, 2], jnp.int32)
    pair_hi = jnp.array([1, 2, 3, 2, 3, 3], jnp.int32)
    grp, pair = tcls // N_PAIRS, tcls % N_PAIRS
    per_tile = MOE_TILE // ZERO_TOKENS
    zfrom = jnp.concatenate([(tile_start * MOE_TILE + cnt) // ZERO_TOKENS, tile_end[-1:] * per_tile])
    zto = jnp.concatenate([tile_end * per_tile, jnp.full((1,), tiles.shape[0] * per_tile, jnp.int32)])
    return (pos, grp * EXPERTS_PER_GROUP + pair_lo[pair], grp * EXPERTS_PER_GROUP + pair_hi[pair], rows,
            zfrom.astype(jnp.int32), zto.astype(jnp.int32))


def kernel(x, meta_tokens, w_in, hg_lower_bound, hg_norm_g, sb_norm_g, w_out, ln1_g, ln1_b,
           w_router_group, b_router_group, w_router_expert, b_router_expert,
           w_exp_gate, w_exp_up, w_exp_down, ln2_g, ln2_b):
    batch, seq, d = x.shape
    assert d == D_MODEL and seq % (HG_CHUNK * HG_STEP_CHUNKS) == 0 and seq % SB_TILE == 0
    assert w_in.shape[0] == DEPTH == 1
    n = batch * seq
    assert n % PERM_TOKENS == 0 and n % MOE_TILE == 0
    x2d = x.reshape(n, d)

    w_in_b = w_in[0].astype(BF16)
    w_out_b = w_out[0].astype(BF16)
    lb = jnp.cumsum(jax.nn.softmax(hg_lower_bound.astype(F32), axis=0), axis=0)[0].reshape(1, HG_WIDTH)
    n_logits = N_GROUPS + N_EXPERTS
    wr = jnp.concatenate([w_router_group[0], w_router_expert[0].reshape(d, N_EXPERTS),
                          jnp.zeros((d, LANES - n_logits), F32)], axis=1)
    wr_hi = wr.astype(BF16)
    wr_mid = (wr - wr_hi.astype(F32)).astype(BF16)
    br = jnp.concatenate([b_router_group[0], b_router_expert[0].reshape(N_EXPERTS),
                          jnp.zeros((LANES - n_logits,), F32)]).reshape(LANES, 1)
    meta_pad = jnp.concatenate([jnp.zeros((META_TILE - N_META, d), F32), meta_tokens.astype(F32)], axis=0)

    hq, hf, hi, hg, sq, sk, sv = _proj(x2d, w_in_b, 512)
    _, mf, mi, _, _, mk, mv = _proj(meta_pad, w_in_b, META_TILE)

    o_hg = _hgrn(hq, hf, hi, hg, mf, mi, lb, hg_norm_g[0].reshape(1, HG_WIDTH), batch, seq)
    o_sb = _stickbreak(sq, sk, sv, mk, mv, sb_norm_g[0].reshape(1, SB_WIDTH), batch, seq)

    hx, cls, rank, counts = _outproj(x2d, o_hg, o_sb, w_out_b, ln1_g[0].reshape(1, d), ln1_b[0].reshape(1, d),
                                     (wr_hi, wr_mid), br, 512)
    pos, tile_ea, tile_eb, tile_rows, zfrom, zto = _route(cls, rank, counts, n)
    xs = _scatter_tokens(pos, zfrom, zto, hx, tile_ea.shape[0] * MOE_TILE)
    ys = _moe(tile_ea, tile_eb, tile_rows, xs, w_exp_gate[0].astype(BF16), w_exp_up[0].astype(BF16),
              w_exp_down[0].astype(BF16), ln2_g[0].reshape(1, d), ln2_b[0].reshape(1, d))
    return _gather_tokens(pos, ys).reshape(batch, seq, d)
```

```python
import functools

import jax
import jax.numpy as jnp
from jax import lax
from jax.experimental import pallas as pl
from jax.experimental.pallas import tpu as pltpu

F32 = jnp.float32
BF16 = jnp.bfloat16

D_MODEL = 1024
N_META = 16
HG_WIDTH = 512
HG_HEADS = 4
HG_DK = 128
SB_WIDTH = 512
SB_DH = 64
SB_PAIRS = 4
N_GROUPS = 4
EXPERTS_PER_GROUP = 4
N_EXPERTS = 16
D_EXPERT = 512
DEPTH = 1
ALPHA = (2 * DEPTH) ** 0.25
LN_EPS = 1e-5
RMS_EPS = 1e-6
N_SPLITS = 7
PROJ_DTYPES = (F32, F32, F32, F32, BF16, BF16, BF16)

N_PAIRS = 6
N_CLASSES = N_GROUPS * N_PAIRS
CLS_ROWS = 32
H_ROWS = D_MODEL // 128
SLAB = 2 * H_ROWS
MOE_TILE = 512
PERM_TOKENS = 1024
ZERO_TOKENS = 64
ISSUE_UNROLL = 8

LANES = 128
HG_CHUNK = 128
HG_STEP_CHUNKS = 4
SB_TILE = 128
META_TILE = 128
SKIP_LOG = -88.0
VMEM_LIMIT = 56 * 1024 * 1024

NT_DIMS = (((1,), (1,)), ((), ()))


def _dot(a, b):
    return jnp.dot(a, b, preferred_element_type=F32)


def _dot_nt(a, b):
    return lax.dot_general(a, b, NT_DIMS, preferred_element_type=F32)


def _split3(x):
    hi = x.astype(BF16)
    r1 = x - hi.astype(F32)
    mid = r1.astype(BF16)
    lo = (r1 - mid.astype(F32)).astype(BF16)
    return hi, mid, lo


def _split2(x):
    hi = x.astype(BF16)
    return hi, (x - hi.astype(F32)).astype(BF16)


def _proj_kernel(x_ref, w_ref, *out_refs):
    xb = x_ref[...].astype(BF16)
    for j, o_ref in enumerate(out_refs):
        o_ref[...] = _dot(xb, w_ref[:, j * 512:(j + 1) * 512]).astype(o_ref.dtype)


def _proj(x2d, w_bf16, tm):
    n = x2d.shape[0]
    out_shape = [jax.ShapeDtypeStruct((n, 512), dt) for dt in PROJ_DTYPES]
    return pl.pallas_call(
        _proj_kernel,
        grid=(n // tm,),
        in_specs=[pl.BlockSpec((tm, D_MODEL), lambda i: (i, 0)),
                  pl.BlockSpec((D_MODEL, N_SPLITS * 512), lambda i: (0, 0))],
        out_specs=[pl.BlockSpec((tm, 512), lambda i: (i, 0)) for _ in range(N_SPLITS)],
        out_shape=out_shape,
        compiler_params=pltpu.CompilerParams(dimension_semantics=("parallel",),
                                             vmem_limit_bytes=VMEM_LIMIT),
        name="proj",
    )(x2d, w_bf16)


def _hgrn_chunk(zfs, zis, lbs, sts, tri, zqs=None, causal=None):
    n = len(zfs)
    c = zfs[0].shape[0]
    ks = [(1.0 - lbs[h]) * jax.nn.sigmoid(-zfs[h]) for h in range(n)]
    splits = [_split3(jnp.log(1.0 - k)) for k in ks]
    bcs = [_dot(tri, hi) + _dot(tri, mid) + _dot(tri, lo) for hi, mid, lo in splits]
    b_last = [bc[c - 1:c, :] for bc in bcs]
    os = None
    if zqs is not None:
        qs = [zq * jax.nn.sigmoid(zq) for zq in zqs]
        refs = [bc[c // 2 - 1:c // 2, :] for bc in bcs]
        qes = [(qs[h] * jnp.exp(bcs[h] - refs[h])).astype(BF16) for h in range(n)]
        kes = [(ks[h] * jnp.exp(refs[h] - bcs[h])).astype(BF16) for h in range(n)]
        qbs = [(qs[h] * jnp.exp(bcs[h])).astype(BF16) for h in range(n)]
        scs = [jnp.where(causal, _dot_nt(qes[h], kes[h]), 0.0).astype(BF16) for h in range(n)]
        os = [_dot(scs[h], zis[h].astype(BF16)) + _dot_nt(qbs[h], sts[h].astype(BF16)) for h in range(n)]
    kls = [(ks[h] * jnp.exp(b_last[h] - bcs[h])).astype(BF16) for h in range(n)]
    new_sts = [sts[h] * jnp.exp(b_last[h]) + _dot(zis[h].T.astype(BF16), kls[h]) for h in range(n)]
    return new_sts, os


def _hgrn_kernel(q_ref, f_ref, i_ref, g_ref, mf_ref, mi_ref, lb_ref, ng_ref, o_ref, st_ref):
    c = HG_CHUNK
    row = lax.broadcasted_iota(jnp.int32, (c, c), 0)
    col = lax.broadcasted_iota(jnp.int32, (c, c), 1)
    causal = col <= row
    tri = causal.astype(BF16)
    sls = [slice(h * HG_DK, (h + 1) * HG_DK) for h in range(HG_HEADS)]
    lbs = [lb_ref[:, sl] for sl in sls]

    @pl.when(pl.program_id(1) == 0)
    def _():
        zero = jnp.zeros((HG_DK, HG_DK), F32)
        sts, _ = _hgrn_chunk([mf_ref[:, sl] for sl in sls], [mi_ref[:, sl] for sl in sls], lbs,
                             [zero] * HG_HEADS, tri)
        for h in range(HG_HEADS):
            st_ref[h] = sts[h]

    sts = [st_ref[h] for h in range(HG_HEADS)]
    for j in range(HG_STEP_CHUNKS):
        rows = slice(j * c, (j + 1) * c)
        sts, os = _hgrn_chunk([f_ref[rows, sl] for sl in sls], [i_ref[rows, sl] for sl in sls], lbs, sts, tri,
                              zqs=[q_ref[rows, sl] for sl in sls], causal=causal)
        for h, sl in enumerate(sls):
            ms = jnp.mean(os[h] * os[h], axis=-1, keepdims=True)
            y = os[h] * lax.rsqrt(ms + RMS_EPS) * ng_ref[:, sl] * jax.nn.sigmoid(g_ref[rows, sl])
            o_ref[rows, sl] = y.astype(o_ref.dtype)
    for h in range(HG_HEADS):
        st_ref[h] = sts[h]


def _hgrn(hq, hf, hi, hg, mf, mi, lb, ng, batch, seq):
    step_rows = HG_CHUNK * HG_STEP_CHUNKS
    nc = seq // step_rows
    blk = pl.BlockSpec((step_rows, HG_WIDTH), lambda b, c: (b * nc + c, 0))
    meta = pl.BlockSpec((META_TILE, HG_WIDTH), lambda b, c: (0, 0))
    vec = pl.BlockSpec((1, HG_WIDTH), lambda b, c: (0, 0))
    return pl.pallas_call(
        _hgrn_kernel,
        grid=(batch, nc),
        in_specs=[blk, blk, blk, blk, meta, meta, vec, vec],
        out_specs=blk,
        out_shape=jax.ShapeDtypeStruct((batch * seq, HG_WIDTH), BF16),
        scratch_shapes=[pltpu.VMEM((HG_HEADS, HG_DK, HG_DK), F32)],
        compiler_params=pltpu.CompilerParams(dimension_semantics=("parallel", "arbitrary"),
                                             vmem_limit_bytes=VMEM_LIMIT),
        name="hgrn2",
    )(hq, hf, hi, hg, mf, mi, lb, ng)


def _sb_tiles(qs, tiles, rev, carries):
    n = len(qs)
    chains = [(j, p) for j in range(len(tiles)) for p in range(n)]
    zs = {c: _dot_nt(qs[c[1]], tiles[c[0]][0][c[1]]) for c in chains}
    ls_pos, lks = {}, {}
    for c in chains:
        z, mask = zs[c], tiles[c[0]][2]
        sp = jnp.log(1.0 + jnp.exp(-jnp.abs(z)))
        ls_pos[c] = jnp.minimum(z, 0.0) - sp
        lk = ls_pos[c] - z
        lks[c] = lk if mask is None else jnp.where(mask, lk, 0.0)
    splits = {c: _split2(lks[c]) for c in chains}
    css = {c: _dot(splits[c][0], rev) + _dot(splits[c][1], rev) for c in chains}
    sums = {c: jnp.sum(lks[c], axis=-1, keepdims=True) for c in chains}
    parts = [None] * n
    carries = list(carries)
    for j, p in chains:
        mask = tiles[j][2]
        a = jnp.exp(ls_pos[j, p] + css[j, p] + carries[p])
        if mask is not None:
            a = jnp.where(mask, a, 0.0)
        part = _dot(a.astype(BF16), tiles[j][1][p])
        parts[p] = part if parts[p] is None else parts[p] + part
        carries[p] = carries[p] + sums[j, p]
    return parts, carries


def _sb_kernel(q_ref, k_ref, v_ref, mk_ref, mv_ref, ng_ref, o_ref, acc_ref, car_ref):
    t = SB_TILE
    qi = pl.program_id(1)
    lane = lax.broadcasted_iota(jnp.int32, (1, LANES), 1)
    lo = lane < SB_DH
    row = lax.broadcasted_iota(jnp.int32, (2 * t, t), 0)
    col = lax.broadcasted_iota(jnp.int32, (2 * t, t), 1)
    rev = (lax.broadcasted_iota(jnp.int32, (t, t), 0)
           > lax.broadcasted_iota(jnp.int32, (t, t), 1)).astype(BF16)
    diag_mask = col < jnp.where(row >= t, row - t, row)
    meta_mask = col >= META_TILE - N_META

    def pair(ref, s, p):
        return ref[pl.ds(s, t), p * LANES:(p + 1) * LANES]

    qs = []
    for p in range(SB_PAIRS):
        q = q_ref[:, p * LANES:(p + 1) * LANES] * (SB_DH ** -0.5)
        qs.append(jnp.concatenate([jnp.where(lo, q, 0.0), jnp.where(lo, 0.0, q)], axis=0).astype(BF16))

    def x_tile(s, mask):
        return ([pair(k_ref, s, p) for p in range(SB_PAIRS)], [pair(v_ref, s, p) for p in range(SB_PAIRS)], mask)

    def meta_pair(ref, p):
        return ref[:, p * LANES:(p + 1) * LANES]

    def alive_of(carries):
        m = carries[0]
        for c in carries[1:]:
            m = jnp.maximum(m, c)
        return jnp.max(m) > SKIP_LOG

    has_prev = jnp.full((t, LANES), qi, jnp.int32) > 0
    prev = pl.multiple_of(jnp.maximum(qi - 1, 0) * t, t)
    prev_tile = ([jnp.where(has_prev, pair(k_ref, prev, p), meta_pair(mk_ref, p)) for p in range(SB_PAIRS)],
                 [jnp.where(has_prev, pair(v_ref, prev, p), meta_pair(mv_ref, p)) for p in range(SB_PAIRS)],
                 jnp.logical_or(jnp.full((2 * t, t), qi, jnp.int32) > 0, meta_mask))
    zero = jnp.zeros((2 * t, 1), F32)
    accs, carries = _sb_tiles(qs, [x_tile(pl.multiple_of(qi * t, t), diag_mask), prev_tile], rev,
                              [zero] * SB_PAIRS)

    def cond(state):
        return jnp.logical_and(state[0] >= 0, state[1])

    def body(state):
        kt, _, accs, carries = state
        parts, carries = _sb_tiles(qs, [x_tile(pl.multiple_of(kt * t, t), None)], rev, carries)
        return kt - 1, alive_of(carries), [a + b for a, b in zip(accs, parts)], carries

    _, alive, accs, carries = lax.while_loop(cond, body, (qi - 2, alive_of(carries), accs, carries))
    for p in range(SB_PAIRS):
        acc_ref[p] = accs[p]
        car_ref[p] = carries[p]

    @pl.when(jnp.logical_and(alive, qi > 0))
    def _():
        meta_tile = ([meta_pair(mk_ref, p) for p in range(SB_PAIRS)],
                     [meta_pair(mv_ref, p) for p in range(SB_PAIRS)], meta_mask)
        parts, _ = _sb_tiles(qs, [meta_tile], rev, [car_ref[p] for p in range(SB_PAIRS)])
        for p in range(SB_PAIRS):
            acc_ref[p] += parts[p]

    for p in range(SB_PAIRS):
        sl = slice(p * LANES, (p + 1) * LANES)
        o = jnp.where(lo, acc_ref[p, :t, :], acc_ref[p, t:, :])
        sq = o * o
        ms_lo = jnp.sum(jnp.where(lo, sq, 0.0), axis=-1, keepdims=True) * (1.0 / SB_DH)
        ms_hi = jnp.sum(jnp.where(lo, 0.0, sq), axis=-1, keepdims=True) * (1.0 / SB_DH)
        inv = jnp.where(lo, lax.rsqrt(ms_lo + RMS_EPS), lax.rsqrt(ms_hi + RMS_EPS))
        o_ref[:, sl] = (o * inv * ng_ref[:, sl]).astype(o_ref.dtype)


def _stickbreak(sq, sk, sv, mk, mv, ng, batch, seq):
    nq = seq // SB_TILE
    qblk = pl.BlockSpec((SB_TILE, SB_WIDTH), lambda b, i: (b * nq + i, 0))
    kvblk = pl.BlockSpec((seq, SB_WIDTH), lambda b, i: (b, 0))
    mblk = pl.BlockSpec((META_TILE, SB_WIDTH), lambda b, i: (0, 0))
    gblk = pl.BlockSpec((1, SB_WIDTH), lambda b, i: (0, 0))
    return pl.pallas_call(
        _sb_kernel,
        grid=(batch, nq),
        in_specs=[qblk, kvblk, kvblk, mblk, mblk, gblk],
        out_specs=qblk,
        out_shape=jax.ShapeDtypeStruct((batch * seq, SB_WIDTH), BF16),
        scratch_shapes=[pltpu.VMEM((SB_PAIRS, 2 * SB_TILE, LANES), F32),
                        pltpu.VMEM((SB_PAIRS, 2 * SB_TILE, 1), F32)],
        compiler_params=pltpu.CompilerParams(dimension_semantics=("parallel", "arbitrary"),
                                             vmem_limit_bytes=VMEM_LIMIT),
        name="stickbreak",
    )(sq, sk, sv, mk, mv, ng)


def _layer_norm(x, g, b):
    mu = jnp.mean(x, axis=-1, keepdims=True)
    xc = x - mu
    var = jnp.mean(xc * xc, axis=-1, keepdims=True)
    return xc * lax.rsqrt(var + LN_EPS) * g + b


def _argmax_first(rows):
    best = rows[0]
    idx = jnp.zeros(best.shape, jnp.int32)
    for j in range(1, len(rows)):
        better = rows[j] > best
        idx = jnp.where(better, j, idx)
        best = jnp.where(better, rows[j], best)
    return best, idx


def _outproj_kernel(x_ref, ohg_ref, osb_ref, w_ref, g_ref, b_ref, wr_hi_ref, wr_mid_ref, br_ref,
                    hx_ref, cls_ref, rank_ref, cnt_ref, run_ref):
    tm = x_ref.shape[0]

    @pl.when(pl.program_id(0) == 0)
    def _():
        run_ref[...] = jnp.zeros(run_ref.shape, F32)

    y = _dot(jnp.concatenate([ohg_ref[...], osb_ref[...]], axis=1), w_ref[...])
    h1 = _layer_norm(ALPHA * x_ref[...] + y, g_ref[...], b_ref[...])
    for s in range(H_ROWS):
        hx_ref[pl.ds(s, tm, stride=SLAB), :] = h1[:, s * LANES:(s + 1) * LANES]

    h_hi, h_mid = _split2(h1)
    w_hi = wr_hi_ref[...]
    lg = _dot(h_hi, w_hi) + _dot(h_mid, w_hi) + _dot(h_hi, wr_mid_ref[...])
    lt = lg.T + br_ref[...]

    lgs = [lt[g:g + 1, :] for g in range(N_GROUPS)]
    best, grp = _argmax_first(lgs)
    denom = jnp.exp(lgs[0] - best)
    for g in range(1, N_GROUPS):
        denom = denom + jnp.exp(lgs[g] - best)
    p_grp = 1.0 / denom

    li = []
    for e in range(EXPERTS_PER_GROUP):
        acc = jnp.zeros_like(best)
        for g in range(N_GROUPS):
            r = N_GROUPS + g * EXPERTS_PER_GROUP + e
            acc = acc + jnp.where(grp == g, lt[r:r + 1, :], 0.0)
        li.append(acc)
    v1, i1 = _argmax_first(li)
    neg = jnp.full_like(v1, -jnp.inf)
    v2, i2 = _argmax_first([jnp.where(i1 == e, neg, li[e]) for e in range(EXPERTS_PER_GROUP)])
    ex = jnp.exp(v2 - v1)
    w1 = 1.0 / (1.0 + ex)
    w2 = ex / (1.0 + ex)

    first = i1 < i2
    e_lo = jnp.where(first, i1, i2)
    e_hi = jnp.where(first, i2, i1)
    pair = jnp.where(e_lo == 0, e_hi - 1, jnp.where(e_lo == 1, e_hi + 1, 5))
    cls = grp * N_PAIRS + pair
    g_lo = p_grp * jnp.where(first, w1, w2)
    g_hi = p_grp * jnp.where(first, w2, w1)
    sub = lax.broadcasted_iota(jnp.int32, (LANES, tm), 0)
    gates = jnp.where(sub == 0, g_lo, jnp.where(sub == 1, g_hi, 0.0)).T
    for s in range(H_ROWS, SLAB):
        hx_ref[pl.ds(s, tm, stride=SLAB), :] = gates

    onehot = lax.broadcasted_iota(jnp.int32, (CLS_ROWS, tm), 0) == cls
    oh = onehot.astype(BF16)
    earlier = (lax.broadcasted_iota(jnp.int32, (tm, tm), 0)
               < lax.broadcasted_iota(jnp.int32, (tm, tm), 1)).astype(BF16)
    run = run_ref[...]
    rank = jnp.sum(jnp.where(onehot, _dot(oh, earlier) + run, 0.0), axis=0, keepdims=True)
    cls_ref[...] = cls
    rank_ref[...] = rank.astype(jnp.int32)
    run = run + jnp.sum(oh.astype(F32), axis=1, keepdims=True)
    run_ref[...] = run
    cnt_ref[...] = jnp.broadcast_to(run, cnt_ref.shape)


def _outproj(x2d, ohg, osb, w_bf16, g, b, wr_split, br, tm):
    n = x2d.shape[0]
    const = lambda shape: pl.BlockSpec(shape, lambda i: (0, 0))
    return pl.pallas_call(
        _outproj_kernel,
        grid=(n // tm,),
        in_specs=[pl.BlockSpec((tm, D_MODEL), lambda i: (i, 0)),
                  pl.BlockSpec((tm, HG_WIDTH), lambda i: (i, 0)),
                  pl.BlockSpec((tm, SB_WIDTH), lambda i: (i, 0)),
                  const((D_MODEL, D_MODEL)), const((1, D_MODEL)), const((1, D_MODEL)),
                  const((D_MODEL, LANES)), const((D_MODEL, LANES)), const((LANES, 1))],
        out_specs=[pl.BlockSpec((tm * SLAB, LANES), lambda i: (i, 0)),
                   pl.BlockSpec((1, tm), lambda i: (0, i)),
                   pl.BlockSpec((1, tm), lambda i: (0, i)),
                   const((CLS_ROWS, LANES))],
        out_shape=[jax.ShapeDtypeStruct((n * SLAB, LANES), F32),
                   jax.ShapeDtypeStruct((1, n), jnp.int32),
                   jax.ShapeDtypeStruct((1, n), jnp.int32),
                   jax.ShapeDtypeStruct((CLS_ROWS, LANES), F32)],
        scratch_shapes=[pltpu.VMEM((CLS_ROWS, 1), F32)],
        compiler_params=pltpu.CompilerParams(dimension_semantics=("arbitrary",),
                                             vmem_limit_bytes=VMEM_LIMIT),
        name="outproj",
    )(x2d, ohg, osb, w_bf16, g, b, *wr_split, br)


def _slab(ref, token, rows):
    return ref.at[pl.ds(pl.multiple_of(token * rows, rows), rows)]


def _scatter_kernel(pos_ref, zfrom_ref, zto_ref, src_ref, dst_ref, zero_ref, sem, zsem):
    base = pl.program_id(0) * PERM_TOKENS

    @pl.when(pl.program_id(0) == 0)
    def _():
        zero_ref[...] = jnp.zeros(zero_ref.shape, F32)

        def fill(k):
            return pltpu.make_async_copy(zero_ref, _slab(dst_ref, k, ZERO_TOKENS * SLAB), zsem)

        def each_chunk(do):
            def per_range(c, carry):
                def per_chunk(k, carry):
                    do(fill(k))
                    return carry
                return lax.fori_loop(zfrom_ref[c], zto_ref[c], per_chunk, carry)
            lax.fori_loop(0, zfrom_ref.shape[0], per_range, 0)

        each_chunk(lambda copy: copy.start())
        each_chunk(lambda copy: copy.wait())

    def issue(g, carry):
        for u in range(ISSUE_UNROLL):
            r = g * ISSUE_UNROLL + u
            pltpu.make_async_copy(_slab(src_ref, r, SLAB), _slab(dst_ref, pos_ref[base + r], SLAB),
                                  sem).start(priority=u % 2)
        return carry

    lax.fori_loop(0, PERM_TOKENS // ISSUE_UNROLL, issue, 0)
    pltpu.make_async_copy(src_ref, dst_ref.at[pl.ds(0, PERM_TOKENS * SLAB)], sem).wait()


def _scatter_tokens(pos, zfrom, zto, hx, sorted_tokens):
    n = pos.shape[0]
    return pl.pallas_call(
        _scatter_kernel,
        grid_spec=pltpu.PrefetchScalarGridSpec(
            num_scalar_prefetch=3,
            grid=(n // PERM_TOKENS,),
            in_specs=[pl.BlockSpec((PERM_TOKENS * SLAB, LANES), lambda i, pos, zf, zt: (i, 0))],
            out_specs=pl.BlockSpec(memory_space=pl.ANY),
            scratch_shapes=[pltpu.VMEM((ZERO_TOKENS * SLAB, LANES), F32),
                            pltpu.SemaphoreType.DMA(()), pltpu.SemaphoreType.DMA(())]),
        out_shape=jax.ShapeDtypeStruct((sorted_tokens * SLAB, LANES), F32),
        compiler_params=pltpu.CompilerParams(dimension_semantics=("arbitrary",),
                                             vmem_limit_bytes=VMEM_LIMIT),
        name="scatter_tokens",
    )(pos, zfrom, zto, hx)


def _gather_kernel(pos_ref, src_ref, o_ref, buf_ref, sem):
    base = pl.program_id(0) * PERM_TOKENS

    def issue(g, carry):
        for u in range(ISSUE_UNROLL):
            r = g * ISSUE_UNROLL + u
            pltpu.make_async_copy(_slab(src_ref, pos_ref[base + r], H_ROWS), _slab(buf_ref, r, H_ROWS),
                                  sem).start(priority=u % 2)
        return carry

    lax.fori_loop(0, PERM_TOKENS // ISSUE_UNROLL, issue, 0)
    pltpu.make_async_copy(src_ref.at[pl.ds(0, PERM_TOKENS * H_ROWS)], buf_ref, sem).wait()
    for s in range(H_ROWS):
        o_ref[:, s * LANES:(s + 1) * LANES] = buf_ref[pl.ds(s, PERM_TOKENS, stride=H_ROWS), :]


def _gather_tokens(pos, ys):
    n = pos.shape[0]
    return pl.pallas_call(
        _gather_kernel,
        grid_spec=pltpu.PrefetchScalarGridSpec(
            num_scalar_prefetch=1,
            grid=(n // PERM_TOKENS,),
            in_specs=[pl.BlockSpec(memory_space=pl.ANY)],
            out_specs=pl.BlockSpec((PERM_TOKENS, D_MODEL), lambda i, pos: (i, 0)),
            scratch_shapes=[pltpu.VMEM((PERM_TOKENS * H_ROWS, LANES), F32), pltpu.SemaphoreType.DMA(())]),
        out_shape=jax.ShapeDtypeStruct((n, D_MODEL), F32),
        compiler_params=pltpu.CompilerParams(dimension_semantics=("arbitrary",),
                                             vmem_limit_bytes=VMEM_LIMIT),
        name="gather_tokens",
    )(pos, ys)


def _moe_kernel(ea_ref, eb_ref, rows_ref, x_ref, w1a_ref, w3a_ref, w2a_ref, w1b_ref, w3b_ref, w2b_ref,
                g_ref, b_ref, o_ref):
    i = pl.program_id(0)
    t = MOE_TILE

    @pl.when(rows_ref[i] > 0)
    def _():
        h1 = jnp.concatenate([x_ref[pl.ds(s, t, stride=SLAB), :] for s in range(H_ROWS)], axis=1)
        gates = x_ref[pl.ds(H_ROWS, t, stride=SLAB), :]
        hb = h1.astype(BF16)

        def hidden(w1_ref, w3_ref, gate):
            a = _dot(hb, w1_ref[0])
            return (gate * (a * jax.nn.sigmoid(a)) * _dot(hb, w3_ref[0])).astype(BF16)

        y = (_dot(hidden(w1a_ref, w3a_ref, gates[:, 0:1]), w2a_ref[0])
             + _dot(hidden(w1b_ref, w3b_ref, gates[:, 1:2]), w2b_ref[0]))
        out = _layer_norm(ALPHA * h1 + y, g_ref[...], b_ref[...])
        for s in range(H_ROWS):
            o_ref[pl.ds(s, t, stride=H_ROWS), :] = out[:, s * LANES:(s + 1) * LANES]

    @pl.when(rows_ref[i] == 0)
    def _():
        o_ref[...] = jnp.zeros(o_ref.shape, F32)


def _moe(tile_ea, tile_eb, tile_rows, xs, w1, w3, w2, g, b):
    nt = tile_ea.shape[0]
    wa = lambda shape: pl.BlockSpec(shape, lambda i, ea, eb, r: (ea[i], 0, 0))
    wb = lambda shape: pl.BlockSpec(shape, lambda i, ea, eb, r: (eb[i], 0, 0))
    up, down = (1, D_MODEL, D_EXPERT), (1, D_EXPERT, D_MODEL)
    return pl.pallas_call(
        _moe_kernel,
        grid_spec=pltpu.PrefetchScalarGridSpec(
            num_scalar_prefetch=3,
            grid=(nt,),
            in_specs=[pl.BlockSpec((MOE_TILE * SLAB, LANES), lambda i, ea, eb, r: (i, 0)),
                      wa(up), wa(up), wa(down), wb(up), wb(up), wb(down),
                      pl.BlockSpec((1, D_MODEL), lambda i, ea, eb, r: (0, 0)),
                      pl.BlockSpec((1, D_MODEL), lambda i, ea, eb, r: (0, 0))],
            out_specs=pl.BlockSpec((MOE_TILE * H_ROWS, LANES), lambda i, ea, eb, r: (i, 0))),
        out_shape=jax.ShapeDtypeStruct((nt * MOE_TILE * H_ROWS, LANES), F32),
        compiler_params=pltpu.CompilerParams(dimension_semantics=("arbitrary",),
                                             vmem_limit_bytes=VMEM_LIMIT),
        name="moe",
    )(tile_ea, tile_eb, tile_rows, xs, w1, w3, w2, w1, w3, w2, g, b)


def _route(cls, rank, counts, n):
    cnt = counts[:N_CLASSES, 0].astype(jnp.int32)
    ntile = (cnt + MOE_TILE - 1) // MOE_TILE
    tile_end = jnp.cumsum(ntile)
    tile_start = tile_end - ntile
    pos = (tile_start * MOE_TILE)[cls.reshape(n)] + rank.reshape(n)
    tiles = jnp.arange(n // MOE_TILE + N_CLASSES, dtype=jnp.int32)
    tcls = jnp.minimum(jnp.sum((tiles[:, None] >= tile_end[None, :]).astype(jnp.int32), axis=1), N_CLASSES - 1)
    rows = jnp.clip(cnt[tcls] - (tiles - tile_start[tcls]) * MOE_TILE, 0, MOE_TILE)
    rows = jnp.where(tiles < tile_end[-1], rows, 0)
    pair_lo = jnp.array([0, 0, 0, 1, 1, 2], jnp.int32)
    pair_hi = jnp.array([1, 2, 3, 2, 3, 3], jnp.int32)
    grp, pair = tcls // N_PAIRS, tcls % N_PAIRS
    per_tile = MOE_TILE // ZERO_TOKENS
    zfrom = jnp.concatenate([(tile_start * MOE_TILE + cnt) // ZERO_TOKENS, tile_end[-1:] * per_tile])
    zto = jnp.concatenate([tile_end * per_tile, jnp.full((1,), tiles.shape[0] * per_tile, jnp.int32)])
    return (pos, grp * EXPERTS_PER_GROUP + pair_lo[pair], grp * EXPERTS_PER_GROUP + pair_hi[pair], rows,
            zfrom.astype(jnp.int32), zto.astype(jnp.int32))


def kernel(x, meta_tokens, w_in, hg_lower_bound, hg_norm_g, sb_norm_g, w_out, ln1_g, ln1_b,
           w_router_group, b_router_group, w_router_expert, b_router_expert,
           w_exp_gate, w_exp_up, w_exp_down, ln2_g, ln2_b):
    batch, seq, d = x.shape
    assert d == D_MODEL and seq % (HG_CHUNK * HG_STEP_CHUNKS) == 0 and seq % SB_TILE == 0
    assert w_in.shape[0] == DEPTH == 1
    n = batch * seq
    assert n % PERM_TOKENS == 0 and n % MOE_TILE == 0
    x2d = x.reshape(n, d)

    w_in_b = w_in[0].astype(BF16)
    w_out_b = w_out[0].astype(BF16)
    lb = jnp.cumsum(jax.nn.softmax(hg_lower_bound.astype(F32), axis=0), axis=0)[0].reshape(1, HG_WIDTH)
    n_logits = N_GROUPS + N_EXPERTS
    wr = jnp.concatenate([w_router_group[0], w_router_expert[0].reshape(d, N_EXPERTS),
                          jnp.zeros((d, LANES - n_logits), F32)], axis=1)
    wr_hi = wr.astype(BF16)
    wr_mid = (wr - wr_hi.astype(F32)).astype(BF16)
    br = jnp.concatenate([b_router_group[0], b_router_expert[0].reshape(N_EXPERTS),
                          jnp.zeros((LANES - n_logits,), F32)]).reshape(LANES, 1)
    meta_pad = jnp.concatenate([jnp.zeros((META_TILE - N_META, d), F32), meta_tokens.astype(F32)], axis=0)

    hq, hf, hi, hg, sq, sk, sv = _proj(x2d, w_in_b, 512)
    _, mf, mi, _, _, mk, mv = _proj(meta_pad, w_in_b, META_TILE)

    o_hg = _hgrn(hq, hf, hi, hg, mf, mi, lb, hg_norm_g[0].reshape(1, HG_WIDTH), batch, seq)
    o_sb = _stickbreak(sq, sk, sv, mk, mv, sb_norm_g[0].reshape(1, SB_WIDTH), batch, seq)

    hx, cls, rank, counts = _outproj(x2d, o_hg, o_sb, w_out_b, ln1_g[0].reshape(1, d), ln1_b[0].reshape(1, d),
                                     (wr_hi, wr_mid), br, 512)
    pos, tile_ea, tile_eb, tile_rows, zfrom, zto = _route(cls, rank, counts, n)
    xs = _scatter_tokens(pos, zfrom, zto, hx, tile_ea.shape[0] * MOE_TILE)
    ys = _moe(tile_ea, tile_eb, tile_rows, xs, w_exp_gate[0].astype(BF16), w_exp_up[0].astype(BF16),
              w_exp_down[0].astype(BF16), ln2_g[0].reshape(1, d), ln2_b[0].reshape(1, d))
    return _gather_tokens(pos, ys).reshape(batch, seq, d)
```

```python
import functools

import jax
import jax.numpy as jnp
from jax import lax
from jax.experimental import pallas as pl
from jax.experimental.pallas import tpu as pltpu

F32 = jnp.float32
BF16 = jnp.bfloat16

D_MODEL = 1024
N_META = 16
HG_WIDTH = 512
HG_HEADS = 4
HG_DK = 128
SB_WIDTH = 512
SB_DH = 64
SB_PAIRS = 4
N_GROUPS = 4
EXPERTS_PER_GROUP = 4
N_EXPERTS = 16
D_EXPERT = 512
DEPTH = 1
ALPHA = (2 * DEPTH) ** 0.25
LN_EPS = 1e-5
RMS_EPS = 1e-6
N_SPLITS = 7
PROJ_DTYPES = (F32, F32, F32, F32, BF16, BF16, BF16)

N_PAIRS = 6
N_CLASSES = N_GROUPS * N_PAIRS
CLS_ROWS = 32
H_ROWS = D_MODEL // 128
SLAB = 2 * H_ROWS
MOE_TILE = 512
PERM_TOKENS = 1024
ZERO_TOKENS = 64
ISSUE_UNROLL = 8

LANES = 128
HG_CHUNK = 128
HG_STEP_CHUNKS = 4
SB_TILE = 128
META_TILE = 128
SKIP_LOG = -88.0
VMEM_LIMIT = 56 * 1024 * 1024

NT_DIMS = (((1,), (1,)), ((), ()))


def _dot(a, b):
    return jnp.dot(a, b, preferred_element_type=F32)


def _dot_nt(a, b):
    return lax.dot_general(a, b, NT_DIMS, preferred_element_type=F32)


def _split3(x):
    hi = x.astype(BF16)
    r1 = x - hi.astype(F32)
    mid = r1.astype(BF16)
    lo = (r1 - mid.astype(F32)).astype(BF16)
    return hi, mid, lo


def _split2(x):
    hi = x.astype(BF16)
    return hi, (x - hi.astype(F32)).astype(BF16)


def _proj_kernel(x_ref, w_ref, *out_refs):
    xb = x_ref[...].astype(BF16)
    for j, o_ref in enumerate(out_refs):
        o_ref[...] = _dot(xb, w_ref[:, j * 512:(j + 1) * 512]).astype(o_ref.dtype)


def _proj(x2d, w_bf16, tm):
    n = x2d.shape[0]
    out_shape = [jax.ShapeDtypeStruct((n, 512), dt) for dt in PROJ_DTYPES]
    return pl.pallas_call(
        _proj_kernel,
        grid=(n // tm,),
        in_specs=[pl.BlockSpec((tm, D_MODEL), lambda i: (i, 0)),
                  pl.BlockSpec((D_MODEL, N_SPLITS * 512), lambda i: (0, 0))],
        out_specs=[pl.BlockSpec((tm, 512), lambda i: (i, 0)) for _ in range(N_SPLITS)],
        out_shape=out_shape,
        compiler_params=pltpu.CompilerParams(dimension_semantics=("parallel",),
                                             vmem_limit_bytes=VMEM_LIMIT),
        name="proj",
    )(x2d, w_bf16)


def _hgrn_chunk(zfs, zis, lbs, sts, tri, zqs=None, causal=None):
    n = len(zfs)
    c = zfs[0].shape[0]
    ks = [(1.0 - lbs[h]) * jax.nn.sigmoid(-zfs[h]) for h in range(n)]
    splits = [_split3(jnp.log(1.0 - k)) for k in ks]
    bcs = [_dot(tri, hi) + _dot(tri, mid) + _dot(tri, lo) for hi, mid, lo in splits]
    b_last = [bc[c - 1:c, :] for bc in bcs]
    os = None
    if zqs is not None:
        qs = [zq * jax.nn.sigmoid(zq) for zq in zqs]
        refs = [bc[c // 2 - 1:c // 2, :] for bc in bcs]
        qes = [(qs[h] * jnp.exp(bcs[h] - refs[h])).astype(BF16) for h in range(n)]
        kes = [(ks[h] * jnp.exp(refs[h] - bcs[h])).astype(BF16) for h in range(n)]
        qbs = [(qs[h] * jnp.exp(bcs[h])).astype(BF16) for h in range(n)]
        scs = [jnp.where(causal, _dot_nt(qes[h], kes[h]), 0.0).astype(BF16) for h in range(n)]
        os = [_dot(scs[h], zis[h].astype(BF16)) + _dot_nt(qbs[h], sts[h].astype(BF16)) for h in range(n)]
    kls = [(ks[h] * jnp.exp(b_last[h] - bcs[h])).astype(BF16) for h in range(n)]
    new_sts = [sts[h] * jnp.exp(b_last[h]) + _dot(zis[h].T.astype(BF16), kls[h]) for h in range(n)]
    return new_sts, os


def _hgrn_kernel(q_ref, f_ref, i_ref, g_ref, mf_ref, mi_ref, lb_ref, ng_ref, o_ref, st_ref):
    c = HG_CHUNK
    row = lax.broadcasted_iota(jnp.int32, (c, c), 0)
    col = lax.broadcasted_iota(jnp.int32, (c, c), 1)
    causal = col <= row
    tri = causal.astype(BF16)
    sls = [slice(h * HG_DK, (h + 1) * HG_DK) for h in range(HG_HEADS)]
    lbs = [lb_ref[:, sl] for sl in sls]

    @pl.when(pl.program_id(1) == 0)
    def _():
        zero = jnp.zeros((HG_DK, HG_DK), F32)
        sts, _ = _hgrn_chunk([mf_ref[:, sl] for sl in sls], [mi_ref[:, sl] for sl in sls], lbs,
                             [zero] * HG_HEADS, tri)
        for h in range(HG_HEADS):
            st_ref[h] = sts[h]

    sts = [st_ref[h] for h in range(HG_HEADS)]
    for j in range(HG_STEP_CHUNKS):
        rows = slice(j * c, (j + 1) * c)
        sts, os = _hgrn_chunk([f_ref[rows, sl] for sl in sls], [i_ref[rows, sl] for sl in sls], lbs, sts, tri,
                              zqs=[q_ref[rows, sl] for sl in sls], causal=causal)
        for h, sl in enumerate(sls):
            ms = jnp.mean(os[h] * os[h], axis=-1, keepdims=True)
            y = os[h] * lax.rsqrt(ms + RMS_EPS) * ng_ref[:, sl] * jax.nn.sigmoid(g_ref[rows, sl])
            o_ref[rows, sl] = y.astype(o_ref.dtype)
    for h in range(HG_HEADS):
        st_ref[h] = sts[h]


def _hgrn(hq, hf, hi, hg, mf, mi, lb, ng, batch, seq):
    step_rows = HG_CHUNK * HG_STEP_CHUNKS
    nc = seq // step_rows
    blk = pl.BlockSpec((step_rows, HG_WIDTH), lambda b, c: (b * nc + c, 0))
    meta = pl.BlockSpec((META_TILE, HG_WIDTH), lambda b, c: (0, 0))
    vec = pl.BlockSpec((1, HG_WIDTH), lambda b, c: (0, 0))
    return pl.pallas_call(
        _hgrn_kernel,
        grid=(batch, nc),
        in_specs=[blk, blk, blk, blk, meta, meta, vec, vec],
        out_specs=blk,
        out_shape=jax.ShapeDtypeStruct((batch * seq, HG_WIDTH), BF16),
        scratch_shapes=[pltpu.VMEM((HG_HEADS, HG_DK, HG_DK), F32)],
        compiler_params=pltpu.CompilerParams(dimension_semantics=("parallel", "arbitrary"),
                                             vmem_limit_bytes=VMEM_LIMIT),
        name="hgrn2",
    )(hq, hf, hi, hg, mf, mi, lb, ng)


def _sb_tiles(qs, tiles, rev, carries):
    n = len(qs)
    chains = [(j, p) for j in range(len(tiles)) for p in range(n)]
    zs = {c: _dot_nt(qs[c[1]], tiles[c[0]][0][c[1]]) for c in chains}
    ls_pos, lks = {}, {}
    for c in chains:
        z, mask = zs[c], tiles[c[0]][2]
        sp = jnp.log(1.0 + jnp.exp(-jnp.abs(z)))
        ls_pos[c] = jnp.minimum(z, 0.0) - sp
        lk = ls_pos[c] - z
        lks[c] = lk if mask is None else jnp.where(mask, lk, 0.0)
    css = {c: _dot(jnp.concatenate(_split2(lks[c]), axis=1), rev) for c in chains}
    sums = {c: jnp.sum(lks[c], axis=-1, keepdims=True) for c in chains}
    parts = [None] * n
    carries = list(carries)
    for j, p in chains:
        mask = tiles[j][2]
        a = jnp.exp(ls_pos[j, p] + css[j, p] + carries[p])
        if mask is not None:
            a = jnp.where(mask, a, 0.0)
        part = _dot(a.astype(BF16), tiles[j][1][p])
        parts[p] = part if parts[p] is None else parts[p] + part
        carries[p] = carries[p] + sums[j, p]
    return parts, carries


def _sb_kernel(q_ref, k_ref, v_ref, mk_ref, mv_ref, ng_ref, o_ref, acc_ref, car_ref):
    t = SB_TILE
    qi = pl.program_id(1)
    lane = lax.broadcasted_iota(jnp.int32, (1, LANES), 1)
    lo = lane < SB_DH
    row = lax.broadcasted_iota(jnp.int32, (2 * t, t), 0)
    col = lax.broadcasted_iota(jnp.int32, (2 * t, t), 1)
    rev = (lax.broadcasted_iota(jnp.int32, (t, t), 0)
           > lax.broadcasted_iota(jnp.int32, (t, t), 1)).astype(BF16)
    rev = jnp.concatenate([rev, rev], axis=0)
    diag_mask = col < jnp.where(row >= t, row - t, row)
    meta_mask = col >= META_TILE - N_META

    def pair(ref, s, p):
        return ref[pl.ds(s, t), p * LANES:(p + 1) * LANES]

    qs = []
    for p in range(SB_PAIRS):
        q = q_ref[:, p * LANES:(p + 1) * LANES] * (SB_DH ** -0.5)
        qs.append(jnp.concatenate([jnp.where(lo, q, 0.0), jnp.where(lo, 0.0, q)], axis=0).astype(BF16))

    def x_tile(s, mask):
        return ([pair(k_ref, s, p) for p in range(SB_PAIRS)], [pair(v_ref, s, p) for p in range(SB_PAIRS)], mask)

    def meta_pair(ref, p):
        return ref[:, p * LANES:(p + 1) * LANES]

    def alive_of(carries):
        m = carries[0]
        for c in carries[1:]:
            m = jnp.maximum(m, c)
        return jnp.max(m) > SKIP_LOG

    has_prev = jnp.full((t, LANES), qi, jnp.int32) > 0
    prev = pl.multiple_of(jnp.maximum(qi - 1, 0) * t, t)
    prev_tile = ([jnp.where(has_prev, pair(k_ref, prev, p), meta_pair(mk_ref, p)) for p in range(SB_PAIRS)],
                 [jnp.where(has_prev, pair(v_ref, prev, p), meta_pair(mv_ref, p)) for p in range(SB_PAIRS)],
                 jnp.logical_or(jnp.full((2 * t, t), qi, jnp.int32) > 0, meta_mask))
    zero = jnp.zeros((2 * t, 1), F32)
    accs, carries = _sb_tiles(qs, [x_tile(pl.multiple_of(qi * t, t), diag_mask), prev_tile], rev,
                              [zero] * SB_PAIRS)

    def cond(state):
        return jnp.logical_and(state[0] >= 0, state[1])

    def body(state):
        kt, _, accs, carries = state
        parts, carries = _sb_tiles(qs, [x_tile(pl.multiple_of(kt * t, t), None)], rev, carries)
        return kt - 1, alive_of(carries), [a + b for a, b in zip(accs, parts)], carries

    _, alive, accs, carries = lax.while_loop(cond, body, (qi - 2, alive_of(carries), accs, carries))
    for p in range(SB_PAIRS):
        acc_ref[p] = accs[p]
        car_ref[p] = carries[p]

    @pl.when(jnp.logical_and(alive, qi > 0))
    def _():
        meta_tile = ([meta_pair(mk_ref, p) for p in range(SB_PAIRS)],
                     [meta_pair(mv_ref, p) for p in range(SB_PAIRS)], meta_mask)
        parts, _ = _sb_tiles(qs, [meta_tile], rev, [car_ref[p] for p in range(SB_PAIRS)])
        for p in range(SB_PAIRS):
            acc_ref[p] += parts[p]

    for p in range(SB_PAIRS):
        sl = slice(p * LANES, (p + 1) * LANES)
        o = jnp.where(lo, acc_ref[p, :t, :], acc_ref[p, t:, :])
        sq = o * o
        ms_lo = jnp.sum(jnp.where(lo, sq, 0.0), axis=-1, keepdims=True) * (1.0 / SB_DH)
        ms_hi = jnp.sum(jnp.where(lo, 0.0, sq), axis=-1, keepdims=True) * (1.0 / SB_DH)
        inv = jnp.where(lo, lax.rsqrt(ms_lo + RMS_EPS), lax.rsqrt(ms_hi + RMS_EPS))
        o_ref[:, sl] = (o * inv * ng_ref[:, sl]).astype(o_ref.dtype)


def _stickbreak(sq, sk, sv, mk, mv, ng, batch, seq):
    nq = seq // SB_TILE
    qblk = pl.BlockSpec((SB_TILE, SB_WIDTH), lambda b, i: (b * nq + i, 0))
    kvblk = pl.BlockSpec((seq, SB_WIDTH), lambda b, i: (b, 0))
    mblk = pl.BlockSpec((META_TILE, SB_WIDTH), lambda b, i: (0, 0))
    gblk = pl.BlockSpec((1, SB_WIDTH), lambda b, i: (0, 0))
    return pl.pallas_call(
        _sb_kernel,
        grid=(batch, nq),
        in_specs=[qblk, kvblk, kvblk, mblk, mblk, gblk],
        out_specs=qblk,
        out_shape=jax.ShapeDtypeStruct((batch * seq, SB_WIDTH), BF16),
        scratch_shapes=[pltpu.VMEM((SB_PAIRS, 2 * SB_TILE, LANES), F32),
                        pltpu.VMEM((SB_PAIRS, 2 * SB_TILE, 1), F32)],
        compiler_params=pltpu.CompilerParams(dimension_semantics=("parallel", "arbitrary"),
                                             vmem_limit_bytes=VMEM_LIMIT),
        name="stickbreak",
    )(sq, sk, sv, mk, mv, ng)


def _layer_norm(x, g, b):
    mu = jnp.mean(x, axis=-1, keepdims=True)
    xc = x - mu
    var = jnp.mean(xc * xc, axis=-1, keepdims=True)
    return xc * lax.rsqrt(var + LN_EPS) * g + b


def _argmax_first(rows):
    best = rows[0]
    idx = jnp.zeros(best.shape, jnp.int32)
    for j in range(1, len(rows)):
        better = rows[j] > best
        idx = jnp.where(better, j, idx)
        best = jnp.where(better, rows[j], best)
    return best, idx


def _outproj_kernel(x_ref, ohg_ref, osb_ref, w_ref, g_ref, b_ref, wr_ref, br_ref,
                    hx_ref, cls_ref, rank_ref, cnt_ref, run_ref):
    tm = x_ref.shape[0]

    @pl.when(pl.program_id(0) == 0)
    def _():
        run_ref[...] = jnp.zeros(run_ref.shape, F32)

    mixed = jnp.concatenate([ohg_ref[...], osb_ref[...]], axis=1)
    half = D_MODEL // 2
    y = jnp.concatenate([_dot(mixed, w_ref[:, :half]), _dot(mixed, w_ref[:, half:])], axis=1)
    h1 = _layer_norm(ALPHA * x_ref[...] + y, g_ref[...], b_ref[...])
    for s in range(H_ROWS):
        hx_ref[pl.ds(s, tm, stride=SLAB), :] = h1[:, s * LANES:(s + 1) * LANES]

    h_hi, h_mid = _split2(h1)
    both = _dot(h_hi, wr_ref[...])
    lg = both[:, :LANES] + both[:, LANES:] + _dot(h_mid, wr_ref[:, :LANES])
    lt = lg.T + br_ref[...]

    lgs = [lt[g:g + 1, :] for g in range(N_GROUPS)]
    best, grp = _argmax_first(lgs)
    denom = jnp.exp(lgs[0] - best)
    for g in range(1, N_GROUPS):
        denom = denom + jnp.exp(lgs[g] - best)
    p_grp = 1.0 / denom

    li = []
    for e in range(EXPERTS_PER_GROUP):
        acc = jnp.zeros_like(best)
        for g in range(N_GROUPS):
            r = N_GROUPS + g * EXPERTS_PER_GROUP + e
            acc = acc + jnp.where(grp == g, lt[r:r + 1, :], 0.0)
        li.append(acc)
    v1, i1 = _argmax_first(li)
    neg = jnp.full_like(v1, -jnp.inf)
    v2, i2 = _argmax_first([jnp.where(i1 == e, neg, li[e]) for e in range(EXPERTS_PER_GROUP)])
    ex = jnp.exp(v2 - v1)
    w1 = 1.0 / (1.0 + ex)
    w2 = ex / (1.0 + ex)

    first = i1 < i2
    e_lo = jnp.where(first, i1, i2)
    e_hi = jnp.where(first, i2, i1)
    pair = jnp.where(e_lo == 0, e_hi - 1, jnp.where(e_lo == 1, e_hi + 1, 5))
    cls = grp * N_PAIRS + pair
    g_lo = p_grp * jnp.where(first, w1, w2)
    g_hi = p_grp * jnp.where(first, w2, w1)
    sub = lax.broadcasted_iota(jnp.int32, (LANES, tm), 0)
    gates = jnp.where(sub == 0, g_lo, jnp.where(sub == 1, g_hi, 0.0)).T
    for s in range(H_ROWS, SLAB):
        hx_ref[pl.ds(s, tm, stride=SLAB), :] = gates

    onehot = lax.broadcasted_iota(jnp.int32, (CLS_ROWS, tm), 0) == cls
    oh = onehot.astype(BF16)
    earlier = (lax.broadcasted_iota(jnp.int32, (tm, tm), 0)
               < lax.broadcasted_iota(jnp.int32, (tm, tm), 1)).astype(BF16)
    run = run_ref[...]
    rank = jnp.sum(jnp.where(onehot, _dot(oh, earlier) + run, 0.0), axis=0, keepdims=True)
    cls_ref[...] = cls
    rank_ref[...] = rank.astype(jnp.int32)
    run = run + jnp.sum(oh.astype(F32), axis=1, keepdims=True)
    run_ref[...] = run
    cnt_ref[...] = jnp.broadcast_to(run, cnt_ref.shape)


def _outproj(x2d, ohg, osb, w_bf16, g, b, wr_split, br, tm):
    n = x2d.shape[0]
    const = lambda shape: pl.BlockSpec(shape, lambda i: (0, 0))
    return pl.pallas_call(
        _outproj_kernel,
        grid=(n // tm,),
        in_specs=[pl.BlockSpec((tm, D_MODEL), lambda i: (i, 0)),
                  pl.BlockSpec((tm, HG_WIDTH), lambda i: (i, 0)),
                  pl.BlockSpec((tm, SB_WIDTH), lambda i: (i, 0)),
                  const((D_MODEL, D_MODEL)), const((1, D_MODEL)), const((1, D_MODEL)),
                  const((D_MODEL, 2 * LANES)), const((LANES, 1))],
        out_specs=[pl.BlockSpec((tm * SLAB, LANES), lambda i: (i, 0)),
                   pl.BlockSpec((1, tm), lambda i: (0, i)),
                   pl.BlockSpec((1, tm), lambda i: (0, i)),
                   const((CLS_ROWS, LANES))],
        out_shape=[jax.ShapeDtypeStruct((n * SLAB, LANES), F32),
                   jax.ShapeDtypeStruct((1, n), jnp.int32),
                   jax.ShapeDtypeStruct((1, n), jnp.int32),
                   jax.ShapeDtypeStruct((CLS_ROWS, LANES), F32)],
        scratch_shapes=[pltpu.VMEM((CLS_ROWS, 1), F32)],
        compiler_params=pltpu.CompilerParams(dimension_semantics=("arbitrary",),
                                             vmem_limit_bytes=VMEM_LIMIT),
        name="outproj",
    )(x2d, ohg, osb, w_bf16, g, b, wr_split, br)


def _slab(ref, token, rows):
    return ref.at[pl.ds(pl.multiple_of(token * rows, rows), rows)]


def _scatter_kernel(pos_ref, zfrom_ref, zto_ref, src_ref, dst_ref, zero_ref, sem, zsem):
    base = pl.program_id(0) * PERM_TOKENS

    @pl.when(pl.program_id(0) == 0)
    def _():
        zero_ref[...] = jnp.zeros(zero_ref.shape, F32)

        def fill(k):
            return pltpu.make_async_copy(zero_ref, _slab(dst_ref, k, ZERO_TOKENS * SLAB), zsem)

        def each_chunk(do):
            def per_range(c, carry):
                def per_chunk(k, carry):
                    do(fill(k))
                    return carry
                return lax.fori_loop(zfrom_ref[c], zto_ref[c], per_chunk, carry)
            lax.fori_loop(0, zfrom_ref.shape[0], per_range, 0)

        each_chunk(lambda copy: copy.start())
        each_chunk(lambda copy: copy.wait())

    def issue(g, carry):
        for u in range(ISSUE_UNROLL):
            r = g * ISSUE_UNROLL + u
            pltpu.make_async_copy(_slab(src_ref, r, SLAB), _slab(dst_ref, pos_ref[base + r], SLAB),
                                  sem).start(priority=u % 2)
        return carry

    lax.fori_loop(0, PERM_TOKENS // ISSUE_UNROLL, issue, 0)
    pltpu.make_async_copy(src_ref, dst_ref.at[pl.ds(0, PERM_TOKENS * SLAB)], sem).wait()


def _scatter_tokens(pos, zfrom, zto, hx, sorted_tokens):
    n = pos.shape[0]
    return pl.pallas_call(
        _scatter_kernel,
        grid_spec=pltpu.PrefetchScalarGridSpec(
            num_scalar_prefetch=3,
            grid=(n // PERM_TOKENS,),
            in_specs=[pl.BlockSpec((PERM_TOKENS * SLAB, LANES), lambda i, pos, zf, zt: (i, 0))],
            out_specs=pl.BlockSpec(memory_space=pl.ANY),
            scratch_shapes=[pltpu.VMEM((ZERO_TOKENS * SLAB, LANES), F32),
                            pltpu.SemaphoreType.DMA(()), pltpu.SemaphoreType.DMA(())]),
        out_shape=jax.ShapeDtypeStruct((sorted_tokens * SLAB, LANES), F32),
        compiler_params=pltpu.CompilerParams(dimension_semantics=("arbitrary",),
                                             vmem_limit_bytes=VMEM_LIMIT),
        name="scatter_tokens",
    )(pos, zfrom, zto, hx)


def _gather_kernel(pos_ref, src_ref, o_ref, buf_ref, sem):
    base = pl.program_id(0) * PERM_TOKENS

    def issue(g, carry):
        for u in range(ISSUE_UNROLL):
            r = g * ISSUE_UNROLL + u
            pltpu.make_async_copy(_slab(src_ref, pos_ref[base + r], H_ROWS), _slab(buf_ref, r, H_ROWS),
                                  sem).start(priority=u % 2)
        return carry

    lax.fori_loop(0, PERM_TOKENS // ISSUE_UNROLL, issue, 0)
    pltpu.make_async_copy(src_ref.at[pl.ds(0, PERM_TOKENS * H_ROWS)], buf_ref, sem).wait()
    for s in range(H_ROWS):
        o_ref[:, s * LANES:(s + 1) * LANES] = buf_ref[pl.ds(s, PERM_TOKENS, stride=H_ROWS), :]


def _gather_tokens(pos, ys):
    n = pos.shape[0]
    return pl.pallas_call(
        _gather_kernel,
        grid_spec=pltpu.PrefetchScalarGridSpec(
            num_scalar_prefetch=1,
            grid=(n // PERM_TOKENS,),
            in_specs=[pl.BlockSpec(memory_space=pl.ANY)],
            out_specs=pl.BlockSpec((PERM_TOKENS, D_MODEL), lambda i, pos: (i, 0)),
            scratch_shapes=[pltpu.VMEM((PERM_TOKENS * H_ROWS, LANES), F32), pltpu.SemaphoreType.DMA(())]),
        out_shape=jax.ShapeDtypeStruct((n, D_MODEL), F32),
        compiler_params=pltpu.CompilerParams(dimension_semantics=("arbitrary",),
                                             vmem_limit_bytes=VMEM_LIMIT),
        name="gather_tokens",
    )(pos, ys)


def _moe_kernel(ea_ref, eb_ref, rows_ref, x_ref, w1a_ref, w3a_ref, w2a_ref, w1b_ref, w3b_ref, w2b_ref,
                g_ref, b_ref, o_ref):
    i = pl.program_id(0)
    t = MOE_TILE

    @pl.when(rows_ref[i] > 0)
    def _():
        h1 = jnp.concatenate([x_ref[pl.ds(s, t, stride=SLAB), :] for s in range(H_ROWS)], axis=1)
        gates = x_ref[pl.ds(H_ROWS, t, stride=SLAB), :]
        hb = h1.astype(BF16)

        def hidden(w1_ref, w3_ref, gate):
            a = _dot(hb, w1_ref[0])
            return (gate * (a * jax.nn.sigmoid(a)) * _dot(hb, w3_ref[0])).astype(BF16)

        y = (_dot(hidden(w1a_ref, w3a_ref, gates[:, 0:1]), w2a_ref[0])
             + _dot(hidden(w1b_ref, w3b_ref, gates[:, 1:2]), w2b_ref[0]))
        out = _layer_norm(ALPHA * h1 + y, g_ref[...], b_ref[...])
        for s in range(H_ROWS):
            o_ref[pl.ds(s, t, stride=H_ROWS), :] = out[:, s * LANES:(s + 1) * LANES]

    @pl.when(rows_ref[i] == 0)
    def _():
        o_ref[...] = jnp.zeros(o_ref.shape, F32)


def _moe(tile_ea, tile_eb, tile_rows, xs, w1, w3, w2, g, b):
    nt = tile_ea.shape[0]
    wa = lambda shape: pl.BlockSpec(shape, lambda i, ea, eb, r: (ea[i], 0, 0))
    wb = lambda shape: pl.BlockSpec(shape, lambda i, ea, eb, r: (eb[i], 0, 0))
    up, down = (1, D_MODEL, D_EXPERT), (1, D_EXPERT, D_MODEL)
    return pl.pallas_call(
        _moe_kernel,
        grid_spec=pltpu.PrefetchScalarGridSpec(
            num_scalar_prefetch=3,
            grid=(nt,),
            in_specs=[pl.BlockSpec((MOE_TILE * SLAB, LANES), lambda i, ea, eb, r: (i, 0)),
                      wa(up), wa(up), wa(down), wb(up), wb(up), wb(down),
                      pl.BlockSpec((1, D_MODEL), lambda i, ea, eb, r: (0, 0)),
                      pl.BlockSpec((1, D_MODEL), lambda i, ea, eb, r: (0, 0))],
            out_specs=pl.BlockSpec((MOE_TILE * H_ROWS, LANES), lambda i, ea, eb, r: (i, 0))),
        out_shape=jax.ShapeDtypeStruct((nt * MOE_TILE * H_ROWS, LANES), F32),
        compiler_params=pltpu.CompilerParams(dimension_semantics=("arbitrary",),
                                             vmem_limit_bytes=VMEM_LIMIT),
        name="moe",
    )(tile_ea, tile_eb, tile_rows, xs, w1, w3, w2, w1, w3, w2, g, b)


def _route(cls, rank, counts, n):
    cnt = counts[:N_CLASSES, 0].astype(jnp.int32)
    ntile = (cnt + MOE_TILE - 1) // MOE_TILE
    tile_end = jnp.cumsum(ntile)
    tile_start = tile_end - ntile
    pos = (tile_start * MOE_TILE)[cls.reshape(n)] + rank.reshape(n)
    tiles = jnp.arange(n // MOE_TILE + N_CLASSES, dtype=jnp.int32)
    tcls = jnp.minimum(jnp.sum((tiles[:, None] >= tile_end[None, :]).astype(jnp.int32), axis=1), N_CLASSES - 1)
    rows = jnp.clip(cnt[tcls] - (tiles - tile_start[tcls]) * MOE_TILE, 0, MOE_TILE)
    rows = jnp.where(tiles < tile_end[-1], rows, 0)
    pair_lo = jnp.array([0, 0, 0, 1, 1, 2], jnp.int32)
    pair_hi = jnp.array([1, 2, 3, 2, 3, 3], jnp.int32)
    grp, pair = tcls // N_PAIRS, tcls % N_PAIRS
    per_tile = MOE_TILE // ZERO_TOKENS
    zfrom = jnp.concatenate([(tile_start * MOE_TILE + cnt) // ZERO_TOKENS, tile_end[-1:] * per_tile])
    zto = jnp.concatenate([tile_end * per_tile, jnp.full((1,), tiles.shape[0] * per_tile, jnp.int32)])
    return (pos, grp * EXPERTS_PER_GROUP + pair_lo[pair], grp * EXPERTS_PER_GROUP + pair_hi[pair], rows,
            zfrom.astype(jnp.int32), zto.astype(jnp.int32))


def kernel(x, meta_tokens, w_in, hg_lower_bound, hg_norm_g, sb_norm_g, w_out, ln1_g, ln1_b,
           w_router_group, b_router_group, w_router_expert, b_router_expert,
           w_exp_gate, w_exp_up, w_exp_down, ln2_g, ln2_b):
    batch, seq, d = x.shape
    assert d == D_MODEL and seq % (HG_CHUNK * HG_STEP_CHUNKS) == 0 and seq % SB_TILE == 0
    assert w_in.shape[0] == DEPTH == 1
    n = batch * seq
    assert n % PERM_TOKENS == 0 and n % MOE_TILE == 0
    x2d = x.reshape(n, d)

    w_in_b = w_in[0].astype(BF16)
    w_out_b = w_out[0].astype(BF16)
    lb = jnp.cumsum(jax.nn.softmax(hg_lower_bound.astype(F32), axis=0), axis=0)[0].reshape(1, HG_WIDTH)
    n_logits = N_GROUPS + N_EXPERTS
    wr = jnp.concatenate([w_router_group[0], w_router_expert[0].reshape(d, N_EXPERTS),
                          jnp.zeros((d, LANES - n_logits), F32)], axis=1)
    wr_hi = wr.astype(BF16)
    wr_mid = (wr - wr_hi.astype(F32)).astype(BF16)
    br = jnp.concatenate([b_router_group[0], b_router_expert[0].reshape(N_EXPERTS),
                          jnp.zeros((LANES - n_logits,), F32)]).reshape(LANES, 1)
    meta_pad = jnp.concatenate([jnp.zeros((META_TILE - N_META, d), F32), meta_tokens.astype(F32)], axis=0)

    hq, hf, hi, hg, sq, sk, sv = _proj(x2d, w_in_b, 512)
    _, mf, mi, _, _, mk, mv = _proj(meta_pad, w_in_b, META_TILE)

    o_hg = _hgrn(hq, hf, hi, hg, mf, mi, lb, hg_norm_g[0].reshape(1, HG_WIDTH), batch, seq)
    o_sb = _stickbreak(sq, sk, sv, mk, mv, sb_norm_g[0].reshape(1, SB_WIDTH), batch, seq)

    hx, cls, rank, counts = _outproj(x2d, o_hg, o_sb, w_out_b, ln1_g[0].reshape(1, d), ln1_b[0].reshape(1, d),
                                     jnp.concatenate([wr_hi, wr_mid], axis=1), br, 512)
    pos, tile_ea, tile_eb, tile_rows, zfrom, zto = _route(cls, rank, counts, n)
    xs = _scatter_tokens(pos, zfrom, zto, hx, tile_ea.shape[0] * MOE_TILE)
    ys = _moe(tile_ea, tile_eb, tile_rows, xs, w_exp_gate[0].astype(BF16), w_exp_up[0].astype(BF16),
              w_exp_down[0].astype(BF16), ln2_g[0].reshape(1, d), ln2_b[0].reshape(1, d))
    return _gather_tokens(pos, ys).reshape(batch, seq, d)
```

```python
import functools

import jax
import jax.numpy as jnp
from jax import lax
from jax.experimental import pallas as pl
from jax.experimental.pallas import tpu as pltpu

F32 = jnp.float32
BF16 = jnp.bfloat16

D_MODEL = 1024
N_META = 16
HG_WIDTH = 512
HG_HEADS = 4
HG_DK = 128
SB_WIDTH = 512
SB_DH = 64
SB_PAIRS = 4
N_GROUPS = 4
EXPERTS_PER_GROUP = 4
N_EXPERTS = 16
D_EXPERT = 512
DEPTH = 1
ALPHA = (2 * DEPTH) ** 0.25
LN_EPS = 1e-5
RMS_EPS = 1e-6
N_SPLITS = 7
PROJ_DTYPES = (F32, F32, F32, F32, BF16, BF16, BF16)

N_PAIRS = 6
N_CLASSES = N_GROUPS * N_PAIRS
CLS_ROWS = 32
H_ROWS = D_MODEL // 128
MOE_TILE = 512
PERM_TOKENS = 1024
ZERO_TOKENS = 64
ISSUE_UNROLL = 8

LANES = 128
HG_CHUNK = 128
HG_STEP_CHUNKS = 4
SB_TILE = 128
META_TILE = 128
SKIP_LOG = -88.0
VMEM_LIMIT = 56 * 1024 * 1024

NT_DIMS = (((1,), (1,)), ((), ()))


def _dot(a, b):
    return jnp.dot(a, b, preferred_element_type=F32)


def _dot_nt(a, b):
    return lax.dot_general(a, b, NT_DIMS, preferred_element_type=F32)


def _split3(x):
    hi = x.astype(BF16)
    r1 = x - hi.astype(F32)
    mid = r1.astype(BF16)
    lo = (r1 - mid.astype(F32)).astype(BF16)
    return hi, mid, lo


def _split2(x):
    hi = x.astype(BF16)
    return hi, (x - hi.astype(F32)).astype(BF16)


def _proj_kernel(x_ref, w_ref, *out_refs):
    xb = x_ref[...].astype(BF16)
    for j, o_ref in enumerate(out_refs):
        o_ref[...] = _dot(xb, w_ref[:, j * 512:(j + 1) * 512]).astype(o_ref.dtype)


def _proj(x2d, w_bf16, tm):
    n = x2d.shape[0]
    out_shape = [jax.ShapeDtypeStruct((n, 512), dt) for dt in PROJ_DTYPES]
    return pl.pallas_call(
        _proj_kernel,
        grid=(n // tm,),
        in_specs=[pl.BlockSpec((tm, D_MODEL), lambda i: (i, 0)),
                  pl.BlockSpec((D_MODEL, N_SPLITS * 512), lambda i: (0, 0))],
        out_specs=[pl.BlockSpec((tm, 512), lambda i: (i, 0)) for _ in range(N_SPLITS)],
        out_shape=out_shape,
        compiler_params=pltpu.CompilerParams(dimension_semantics=("parallel",),
                                             vmem_limit_bytes=VMEM_LIMIT),
        name="proj",
    )(x2d, w_bf16)


def _hgrn_chunk(zfs, zis, lbs, sts, tri, zqs=None, causal=None):
    n = len(zfs)
    c = zfs[0].shape[0]
    ks = [(1.0 - lbs[h]) * jax.nn.sigmoid(-zfs[h]) for h in range(n)]
    splits = [_split3(jnp.log(1.0 - k)) for k in ks]
    bcs = [_dot(tri, hi) + _dot(tri, mid) + _dot(tri, lo) for hi, mid, lo in splits]
    b_last = [bc[c - 1:c, :] for bc in bcs]
    os = None
    if zqs is not None:
        qs = [zq * jax.nn.sigmoid(zq) for zq in zqs]
        refs = [bc[c // 2 - 1:c // 2, :] for bc in bcs]
        qes = [(qs[h] * jnp.exp(bcs[h] - refs[h])).astype(BF16) for h in range(n)]
        kes = [(ks[h] * jnp.exp(refs[h] - bcs[h])).astype(BF16) for h in range(n)]
        qbs = [(qs[h] * jnp.exp(bcs[h])).astype(BF16) for h in range(n)]
        scs = [jnp.where(causal, _dot_nt(qes[h], kes[h]), 0.0).astype(BF16) for h in range(n)]
        os = [_dot(scs[h], zis[h].astype(BF16)) + _dot_nt(qbs[h], sts[h].astype(BF16)) for h in range(n)]
    kls = [(ks[h] * jnp.exp(b_last[h] - bcs[h])).astype(BF16) for h in range(n)]
    new_sts = [sts[h] * jnp.exp(b_last[h]) + _dot(zis[h].T.astype(BF16), kls[h]) for h in range(n)]
    return new_sts, os


def _hgrn_kernel(q_ref, f_ref, i_ref, g_ref, mf_ref, mi_ref, lb_ref, ng_ref, o_ref, st_ref):
    c = HG_CHUNK
    row = lax.broadcasted_iota(jnp.int32, (c, c), 0)
    col = lax.broadcasted_iota(jnp.int32, (c, c), 1)
    causal = col <= row
    tri = causal.astype(BF16)
    sls = [slice(h * HG_DK, (h + 1) * HG_DK) for h in range(HG_HEADS)]
    lbs = [lb_ref[:, sl] for sl in sls]

    @pl.when(pl.program_id(1) == 0)
    def _():
        zero = jnp.zeros((HG_DK, HG_DK), F32)
        sts, _ = _hgrn_chunk([mf_ref[:, sl] for sl in sls], [mi_ref[:, sl] for sl in sls], lbs,
                             [zero] * HG_HEADS, tri)
        for h in range(HG_HEADS):
            st_ref[h] = sts[h]

    sts = [st_ref[h] for h in range(HG_HEADS)]
    for j in range(HG_STEP_CHUNKS):
        rows = slice(j * c, (j + 1) * c)
        sts, os = _hgrn_chunk([f_ref[rows, sl] for sl in sls], [i_ref[rows, sl] for sl in sls], lbs, sts, tri,
                              zqs=[q_ref[rows, sl] for sl in sls], causal=causal)
        for h, sl in enumerate(sls):
            ms = jnp.mean(os[h] * os[h], axis=-1, keepdims=True)
            y = os[h] * lax.rsqrt(ms + RMS_EPS) * ng_ref[:, sl] * jax.nn.sigmoid(g_ref[rows, sl])
            o_ref[rows, sl] = y.astype(o_ref.dtype)
    for h in range(HG_HEADS):
        st_ref[h] = sts[h]


def _hgrn(hq, hf, hi, hg, mf, mi, lb, ng, batch, seq):
    step_rows = HG_CHUNK * HG_STEP_CHUNKS
    nc = seq // step_rows
    blk = pl.BlockSpec((step_rows, HG_WIDTH), lambda b, c: (b * nc + c, 0))
    meta = pl.BlockSpec((META_TILE, HG_WIDTH), lambda b, c: (0, 0))
    vec = pl.BlockSpec((1, HG_WIDTH), lambda b, c: (0, 0))
    return pl.pallas_call(
        _hgrn_kernel,
        grid=(batch, nc),
        in_specs=[blk, blk, blk, blk, meta, meta, vec, vec],
        out_specs=blk,
        out_shape=jax.ShapeDtypeStruct((batch * seq, HG_WIDTH), BF16),
        scratch_shapes=[pltpu.VMEM((HG_HEADS, HG_DK, HG_DK), F32)],
        compiler_params=pltpu.CompilerParams(dimension_semantics=("parallel", "arbitrary"),
                                             vmem_limit_bytes=VMEM_LIMIT),
        name="hgrn2",
    )(hq, hf, hi, hg, mf, mi, lb, ng)


def _sb_tiles(qs, tiles, rev, carries):
    n = len(qs)
    chains = [(j, p) for j in range(len(tiles)) for p in range(n)]
    zs = {c: _dot_nt(qs[c[1]], tiles[c[0]][0][c[1]]) for c in chains}
    ls_pos, lks = {}, {}
    for c in chains:
        z, mask = zs[c], tiles[c[0]][2]
        sp = jnp.log(1.0 + jnp.exp(-jnp.abs(z)))
        ls_pos[c] = jnp.minimum(z, 0.0) - sp
        lk = ls_pos[c] - z
        lks[c] = lk if mask is None else jnp.where(mask, lk, 0.0)
    css = {c: _dot(jnp.concatenate(_split2(lks[c]), axis=1), rev) for c in chains}
    sums = {c: jnp.sum(lks[c], axis=-1, keepdims=True) for c in chains}
    parts = [None] * n
    carries = list(carries)
    for j, p in chains:
        mask = tiles[j][2]
        a = jnp.exp(ls_pos[j, p] + css[j, p] + carries[p])
        if mask is not None:
            a = jnp.where(mask, a, 0.0)
        part = _dot(a.astype(BF16), tiles[j][1][p])
        parts[p] = part if parts[p] is None else parts[p] + part
        carries[p] = carries[p] + sums[j, p]
    return parts, carries


def _sb_kernel(q_ref, k_ref, v_ref, mk_ref, mv_ref, ng_ref, o_ref, acc_ref, car_ref):
    t = SB_TILE
    qi = pl.program_id(1)
    lane = lax.broadcasted_iota(jnp.int32, (1, LANES), 1)
    lo = lane < SB_DH
    row = lax.broadcasted_iota(jnp.int32, (2 * t, t), 0)
    col = lax.broadcasted_iota(jnp.int32, (2 * t, t), 1)
    rev = (lax.broadcasted_iota(jnp.int32, (t, t), 0)
           > lax.broadcasted_iota(jnp.int32, (t, t), 1)).astype(BF16)
    rev = jnp.concatenate([rev, rev], axis=0)
    diag_mask = col < jnp.where(row >= t, row - t, row)
    meta_mask = col >= META_TILE - N_META

    def pair(ref, s, p):
        return ref[pl.ds(s, t), p * LANES:(p + 1) * LANES]

    qs = []
    for p in range(SB_PAIRS):
        q = q_ref[:, p * LANES:(p + 1) * LANES] * (SB_DH ** -0.5)
        qs.append(jnp.concatenate([jnp.where(lo, q, 0.0), jnp.where(lo, 0.0, q)], axis=0).astype(BF16))

    def x_tile(s, mask):
        return ([pair(k_ref, s, p) for p in range(SB_PAIRS)], [pair(v_ref, s, p) for p in range(SB_PAIRS)], mask)

    def meta_pair(ref, p):
        return ref[:, p * LANES:(p + 1) * LANES]

    def alive_of(carries):
        m = carries[0]
        for c in carries[1:]:
            m = jnp.maximum(m, c)
        return jnp.max(m) > SKIP_LOG

    has_prev = jnp.full((t, LANES), qi, jnp.int32) > 0
    prev = pl.multiple_of(jnp.maximum(qi - 1, 0) * t, t)
    prev_tile = ([jnp.where(has_prev, pair(k_ref, prev, p), meta_pair(mk_ref, p)) for p in range(SB_PAIRS)],
                 [jnp.where(has_prev, pair(v_ref, prev, p), meta_pair(mv_ref, p)) for p in range(SB_PAIRS)],
                 jnp.logical_or(jnp.full((2 * t, t), qi, jnp.int32) > 0, meta_mask))
    zero = jnp.zeros((2 * t, 1), F32)
    accs, carries = _sb_tiles(qs, [x_tile(pl.multiple_of(qi * t, t), diag_mask), prev_tile], rev,
                              [zero] * SB_PAIRS)

    def cond(state):
        return jnp.logical_and(state[0] >= 0, state[1])

    def body(state):
        kt, _, accs, carries = state
        parts, carries = _sb_tiles(qs, [x_tile(pl.multiple_of(kt * t, t), None)], rev, carries)
        return kt - 1, alive_of(carries), [a + b for a, b in zip(accs, parts)], carries

    _, alive, accs, carries = lax.while_loop(cond, body, (qi - 2, alive_of(carries), accs, carries))
    for p in range(SB_PAIRS):
        acc_ref[p] = accs[p]
        car_ref[p] = carries[p]

    @pl.when(jnp.logical_and(alive, qi > 0))
    def _():
        meta_tile = ([meta_pair(mk_ref, p) for p in range(SB_PAIRS)],
                     [meta_pair(mv_ref, p) for p in range(SB_PAIRS)], meta_mask)
        parts, _ = _sb_tiles(qs, [meta_tile], rev, [car_ref[p] for p in range(SB_PAIRS)])
        for p in range(SB_PAIRS):
            acc_ref[p] += parts[p]

    for p in range(SB_PAIRS):
        sl = slice(p * LANES, (p + 1) * LANES)
        o = jnp.where(lo, acc_ref[p, :t, :], acc_ref[p, t:, :])
        sq = o * o
        ms_lo = jnp.sum(jnp.where(lo, sq, 0.0), axis=-1, keepdims=True) * (1.0 / SB_DH)
        ms_hi = jnp.sum(jnp.where(lo, 0.0, sq), axis=-1, keepdims=True) * (1.0 / SB_DH)
        inv = jnp.where(lo, lax.rsqrt(ms_lo + RMS_EPS), lax.rsqrt(ms_hi + RMS_EPS))
        o_ref[:, sl] = (o * inv * ng_ref[:, sl]).astype(o_ref.dtype)


def _stickbreak(sq, sk, sv, mk, mv, ng, batch, seq):
    nq = seq // SB_TILE
    qblk = pl.BlockSpec((SB_TILE, SB_WIDTH), lambda b, i: (b * nq + i, 0))
    kvblk = pl.BlockSpec((seq, SB_WIDTH), lambda b, i: (b, 0))
    mblk = pl.BlockSpec((META_TILE, SB_WIDTH), lambda b, i: (0, 0))
    gblk = pl.BlockSpec((1, SB_WIDTH), lambda b, i: (0, 0))
    return pl.pallas_call(
        _sb_kernel,
        grid=(batch, nq),
        in_specs=[qblk, kvblk, kvblk, mblk, mblk, gblk],
        out_specs=qblk,
        out_shape=jax.ShapeDtypeStruct((batch * seq, SB_WIDTH), BF16),
        scratch_shapes=[pltpu.VMEM((SB_PAIRS, 2 * SB_TILE, LANES), F32),
                        pltpu.VMEM((SB_PAIRS, 2 * SB_TILE, 1), F32)],
        compiler_params=pltpu.CompilerParams(dimension_semantics=("parallel", "arbitrary"),
                                             vmem_limit_bytes=VMEM_LIMIT),
        name="stickbreak",
    )(sq, sk, sv, mk, mv, ng)


def _layer_norm(x, g, b):
    mu = jnp.mean(x, axis=-1, keepdims=True)
    xc = x - mu
    var = jnp.mean(xc * xc, axis=-1, keepdims=True)
    return xc * lax.rsqrt(var + LN_EPS) * g + b


def _argmax_first(rows):
    best = rows[0]
    idx = jnp.zeros(best.shape, jnp.int32)
    for j in range(1, len(rows)):
        better = rows[j] > best
        idx = jnp.where(better, j, idx)
        best = jnp.where(better, rows[j], best)
    return best, idx


def _router_logits(h1, wr_ref):
    h_hi, h_mid = _split2(h1)
    both = _dot(h_hi, wr_ref[...])
    return both[:, :LANES] + both[:, LANES:] + _dot(h_mid, wr_ref[:, :LANES])


def _outproj_kernel(x_ref, ohg_ref, osb_ref, w_ref, g_ref, b_ref, wr_ref, br_ref,
                    hx_ref, cls_ref, rank_ref, cnt_ref, run_ref):
    tm = x_ref.shape[0]

    @pl.when(pl.program_id(0) == 0)
    def _():
        run_ref[...] = jnp.zeros(run_ref.shape, F32)

    mixed = jnp.concatenate([ohg_ref[...], osb_ref[...]], axis=1)
    half = D_MODEL // 2
    y = jnp.concatenate([_dot(mixed, w_ref[:, :half]), _dot(mixed, w_ref[:, half:])], axis=1)
    h1 = _layer_norm(ALPHA * x_ref[...] + y, g_ref[...], b_ref[...])
    for s in range(H_ROWS):
        hx_ref[pl.ds(s, tm, stride=H_ROWS), :] = h1[:, s * LANES:(s + 1) * LANES]

    lt = _router_logits(h1, wr_ref).T + br_ref[...]
    _, grp = _argmax_first([lt[g:g + 1, :] for g in range(N_GROUPS)])

    li = []
    for e in range(EXPERTS_PER_GROUP):
        acc = jnp.where(grp == 0, lt[N_GROUPS + e:N_GROUPS + e + 1, :], 0.0)
        for g in range(1, N_GROUPS):
            r = N_GROUPS + g * EXPERTS_PER_GROUP + e
            acc = acc + jnp.where(grp == g, lt[r:r + 1, :], 0.0)
        li.append(acc)
    v1, i1 = _argmax_first(li)
    neg = jnp.full_like(v1, -jnp.inf)
    _, i2 = _argmax_first([jnp.where(i1 == e, neg, li[e]) for e in range(EXPERTS_PER_GROUP)])

    e_lo = jnp.minimum(i1, i2)
    e_hi = jnp.maximum(i1, i2)
    pair = jnp.where(e_lo == 0, e_hi - 1, jnp.where(e_lo == 1, e_hi + 1, 5))
    cls = grp * N_PAIRS + pair

    onehot = lax.broadcasted_iota(jnp.int32, (CLS_ROWS, tm), 0) == cls
    oh = onehot.astype(BF16)
    earlier = (lax.broadcasted_iota(jnp.int32, (tm, tm), 0)
               < lax.broadcasted_iota(jnp.int32, (tm, tm), 1)).astype(BF16)
    run = run_ref[...]
    rank = jnp.sum(jnp.where(onehot, _dot(oh, earlier) + run, 0.0), axis=0, keepdims=True)
    cls_ref[...] = cls
    rank_ref[...] = rank.astype(jnp.int32)
    run = run + jnp.sum(oh.astype(F32), axis=1, keepdims=True)
    run_ref[...] = run
    cnt_ref[...] = jnp.broadcast_to(run, cnt_ref.shape)


def _outproj(x2d, ohg, osb, w_bf16, g, b, wr_split, br, tm):
    n = x2d.shape[0]
    const = lambda shape: pl.BlockSpec(shape, lambda i: (0, 0))
    return pl.pallas_call(
        _outproj_kernel,
        grid=(n // tm,),
        in_specs=[pl.BlockSpec((tm, D_MODEL), lambda i: (i, 0)),
                  pl.BlockSpec((tm, HG_WIDTH), lambda i: (i, 0)),
                  pl.BlockSpec((tm, SB_WIDTH), lambda i: (i, 0)),
                  const((D_MODEL, D_MODEL)), const((1, D_MODEL)), const((1, D_MODEL)),
                  const((D_MODEL, 2 * LANES)), const((LANES, 1))],
        out_specs=[pl.BlockSpec((tm * H_ROWS, LANES), lambda i: (i, 0)),
                   pl.BlockSpec((1, tm), lambda i: (0, i)),
                   pl.BlockSpec((1, tm), lambda i: (0, i)),
                   const((CLS_ROWS, LANES))],
        out_shape=[jax.ShapeDtypeStruct((n * H_ROWS, LANES), F32),
                   jax.ShapeDtypeStruct((1, n), jnp.int32),
                   jax.ShapeDtypeStruct((1, n), jnp.int32),
                   jax.ShapeDtypeStruct((CLS_ROWS, LANES), F32)],
        scratch_shapes=[pltpu.VMEM((CLS_ROWS, 1), F32)],
        compiler_params=pltpu.CompilerParams(dimension_semantics=("arbitrary",),
                                             vmem_limit_bytes=VMEM_LIMIT),
        name="outproj",
    )(x2d, ohg, osb, w_bf16, g, b, wr_split, br)


def _slab(ref, token, rows):
    return ref.at[pl.ds(pl.multiple_of(token * rows, rows), rows)]


def _scatter_kernel(pos_ref, zfrom_ref, zto_ref, src_ref, dst_ref, zero_ref, sem, zsem):
    base = pl.program_id(0) * PERM_TOKENS

    @pl.when(pl.program_id(0) == 0)
    def _():
        zero_ref[...] = jnp.zeros(zero_ref.shape, F32)

        def fill(k):
            return pltpu.make_async_copy(zero_ref, _slab(dst_ref, k, ZERO_TOKENS * H_ROWS), zsem)

        def each_chunk(do):
            def per_range(c, carry):
                def per_chunk(k, carry):
                    do(fill(k))
                    return carry
                return lax.fori_loop(zfrom_ref[c], zto_ref[c], per_chunk, carry)
            lax.fori_loop(0, zfrom_ref.shape[0], per_range, 0)

        each_chunk(lambda copy: copy.start())
        each_chunk(lambda copy: copy.wait())

    def issue(g, carry):
        for u in range(ISSUE_UNROLL):
            r = g * ISSUE_UNROLL + u
            pltpu.make_async_copy(_slab(src_ref, r, H_ROWS), _slab(dst_ref, pos_ref[base + r], H_ROWS),
                                  sem).start(priority=u % 2)
        return carry

    lax.fori_loop(0, PERM_TOKENS // ISSUE_UNROLL, issue, 0)
    pltpu.make_async_copy(src_ref, dst_ref.at[pl.ds(0, PERM_TOKENS * H_ROWS)], sem).wait()


def _scatter_tokens(pos, zfrom, zto, hx, sorted_tokens):
    n = pos.shape[0]
    return pl.pallas_call(
        _scatter_kernel,
        grid_spec=pltpu.PrefetchScalarGridSpec(
            num_scalar_prefetch=3,
            grid=(n // PERM_TOKENS,),
            in_specs=[pl.BlockSpec((PERM_TOKENS * H_ROWS, LANES), lambda i, pos, zf, zt: (i, 0))],
            out_specs=pl.BlockSpec(memory_space=pl.ANY),
            scratch_shapes=[pltpu.VMEM((ZERO_TOKENS * H_ROWS, LANES), F32),
                            pltpu.SemaphoreType.DMA(()), pltpu.SemaphoreType.DMA(())]),
        out_shape=jax.ShapeDtypeStruct((sorted_tokens * H_ROWS, LANES), F32),
        compiler_params=pltpu.CompilerParams(dimension_semantics=("arbitrary",),
                                             vmem_limit_bytes=VMEM_LIMIT),
        name="scatter_tokens",
    )(pos, zfrom, zto, hx)


def _gather_kernel(pos_ref, src_ref, o_ref, buf_ref, sem):
    base = pl.program_id(0) * PERM_TOKENS

    def issue(g, carry):
        for u in range(ISSUE_UNROLL):
            r = g * ISSUE_UNROLL + u
            pltpu.make_async_copy(_slab(src_ref, pos_ref[base + r], H_ROWS), _slab(buf_ref, r, H_ROWS),
                                  sem).start(priority=u % 2)
        return carry

    lax.fori_loop(0, PERM_TOKENS // ISSUE_UNROLL, issue, 0)
    pltpu.make_async_copy(src_ref.at[pl.ds(0, PERM_TOKENS * H_ROWS)], buf_ref, sem).wait()
    for s in range(H_ROWS):
        o_ref[:, s * LANES:(s + 1) * LANES] = buf_ref[pl.ds(s, PERM_TOKENS, stride=H_ROWS), :]


def _gather_tokens(pos, ys):
    n = pos.shape[0]
    return pl.pallas_call(
        _gather_kernel,
        grid_spec=pltpu.PrefetchScalarGridSpec(
            num_scalar_prefetch=1,
            grid=(n // PERM_TOKENS,),
            in_specs=[pl.BlockSpec(memory_space=pl.ANY)],
            out_specs=pl.BlockSpec((PERM_TOKENS, D_MODEL), lambda i, pos: (i, 0)),
            scratch_shapes=[pltpu.VMEM((PERM_TOKENS * H_ROWS, LANES), F32), pltpu.SemaphoreType.DMA(())]),
        out_shape=jax.ShapeDtypeStruct((n, D_MODEL), F32),
        compiler_params=pltpu.CompilerParams(dimension_semantics=("arbitrary",),
                                             vmem_limit_bytes=VMEM_LIMIT),
        name="gather_tokens",
    )(pos, ys)


def _moe_kernel(ea_ref, eb_ref, rows_ref, x_ref, w1a_ref, w3a_ref, w2a_ref, w1b_ref, w3b_ref, w2b_ref,
                wr_ref, br_ref, g_ref, b_ref, o_ref):
    i = pl.program_id(0)
    t = MOE_TILE

    @pl.when(rows_ref[i] > 0)
    def _():
        h1 = jnp.concatenate([x_ref[pl.ds(s, t, stride=H_ROWS), :] for s in range(H_ROWS)], axis=1)
        hb = h1.astype(BF16)

        lg = _router_logits(h1, wr_ref) + br_ref[...]
        lane = lax.broadcasted_iota(jnp.int32, (1, LANES), 1)
        pick = lambda j: jnp.sum(jnp.where(lane == j, lg, 0.0), axis=-1, keepdims=True)
        l_grp = pick(ea_ref[i] // EXPERTS_PER_GROUP)
        l_a, l_b = pick(N_GROUPS + ea_ref[i]), pick(N_GROUPS + eb_ref[i])
        p_grp = 1.0 / jnp.sum(jnp.where(lane < N_GROUPS, jnp.exp(lg - l_grp), 0.0), axis=-1, keepdims=True)
        gate_a = p_grp / (1.0 + jnp.exp(l_b - l_a))
        gate_b = p_grp / (1.0 + jnp.exp(l_a - l_b))

        def hidden(w1_ref, w3_ref, gate):
            a = _dot(hb, w1_ref[0])
            return (gate * (a * jax.nn.sigmoid(a)) * _dot(hb, w3_ref[0])).astype(BF16)

        y = _dot(hidden(w1a_ref, w3a_ref, gate_a), w2a_ref[0]) + _dot(hidden(w1b_ref, w3b_ref, gate_b), w2b_ref[0])
        out = _layer_norm(ALPHA * h1 + y, g_ref[...], b_ref[...])
        for s in range(H_ROWS):
            o_ref[pl.ds(s, t, stride=H_ROWS), :] = out[:, s * LANES:(s + 1) * LANES]

    @pl.when(rows_ref[i] == 0)
    def _():
        o_ref[...] = jnp.zeros(o_ref.shape, F32)


def _moe(tile_ea, tile_eb, tile_rows, xs, w1, w3, w2, wr, br_row, g, b):
    nt = tile_ea.shape[0]
    wa = lambda shape: pl.BlockSpec(shape, lambda i, ea, eb, r: (ea[i], 0, 0))
    wb = lambda shape: pl.BlockSpec(shape, lambda i, ea, eb, r: (eb[i], 0, 0))
    up, down = (1, D_MODEL, D_EXPERT), (1, D_EXPERT, D_MODEL)
    return pl.pallas_call(
        _moe_kernel,
        grid_spec=pltpu.PrefetchScalarGridSpec(
            num_scalar_prefetch=3,
            grid=(nt,),
            in_specs=[pl.BlockSpec((MOE_TILE * H_ROWS, LANES), lambda i, ea, eb, r: (i, 0)),
                      wa(up), wa(up), wa(down), wb(up), wb(up), wb(down),
                      pl.BlockSpec((D_MODEL, 2 * LANES), lambda i, ea, eb, r: (0, 0)),
                      pl.BlockSpec((1, LANES), lambda i, ea, eb, r: (0, 0)),
                      pl.BlockSpec((1, D_MODEL), lambda i, ea, eb, r: (0, 0)),
                      pl.BlockSpec((1, D_MODEL), lambda i, ea, eb, r: (0, 0))],
            out_specs=pl.BlockSpec((MOE_TILE * H_ROWS, LANES), lambda i, ea, eb, r: (i, 0))),
        out_shape=jax.ShapeDtypeStruct((nt * MOE_TILE * H_ROWS, LANES), F32),
        compiler_params=pltpu.CompilerParams(dimension_semantics=("arbitrary",),
                                             vmem_limit_bytes=VMEM_LIMIT),
        name="moe",
    )(tile_ea, tile_eb, tile_rows, xs, w1, w3, w2, w1, w3, w2, wr, br_row, g, b)


def _route(cls, rank, counts, n):
    cnt = counts[:N_CLASSES, 0].astype(jnp.int32)
    ntile = (cnt + MOE_TILE - 1) // MOE_TILE
    tile_end = jnp.cumsum(ntile)
    tile_start = tile_end - ntile
    pos = (tile_start * MOE_TILE)[cls.reshape(n)] + rank.reshape(n)
    tiles = jnp.arange(n // MOE_TILE + N_CLASSES, dtype=jnp.int32)
    tcls = jnp.minimum(jnp.sum((tiles[:, None] >= tile_end[None, :]).astype(jnp.int32), axis=1), N_CLASSES - 1)
    rows = jnp.clip(cnt[tcls] - (tiles - tile_start[tcls]) * MOE_TILE, 0, MOE_TILE)
    rows = jnp.where(tiles < tile_end[-1], rows, 0)
    pair_lo = jnp.array([0, 0, 0, 1, 1, 2], jnp.int32)
    pair_hi = jnp.array([1, 2, 3, 2, 3, 3], jnp.int32)
    grp, pair = tcls // N_PAIRS, tcls % N_PAIRS
    per_tile = MOE_TILE // ZERO_TOKENS
    zfrom = jnp.concatenate([(tile_start * MOE_TILE + cnt) // ZERO_TOKENS, tile_end[-1:] * per_tile])
    zto = jnp.concatenate([tile_end * per_tile, jnp.full((1,), tiles.shape[0] * per_tile, jnp.int32)])
    return (pos, grp * EXPERTS_PER_GROUP + pair_lo[pair], grp * EXPERTS_PER_GROUP + pair_hi[pair], rows,
            zfrom.astype(jnp.int32), zto.astype(jnp.int32))


def kernel(x, meta_tokens, w_in, hg_lower_bound, hg_norm_g, sb_norm_g, w_out, ln1_g, ln1_b,
           w_router_group, b_router_group, w_router_expert, b_router_expert,
           w_exp_gate, w_exp_up, w_exp_down, ln2_g, ln2_b):
    batch, seq, d = x.shape
    assert d == D_MODEL and seq % (HG_CHUNK * HG_STEP_CHUNKS) == 0 and seq % SB_TILE == 0
    assert w_in.shape[0] == DEPTH == 1
    n = batch * seq
    assert n % PERM_TOKENS == 0 and n % MOE_TILE == 0
    x2d = x.reshape(n, d)

    w_in_b = w_in[0].astype(BF16)
    w_out_b = w_out[0].astype(BF16)
    lb = jnp.cumsum(jax.nn.softmax(hg_lower_bound.astype(F32), axis=0), axis=0)[0].reshape(1, HG_WIDTH)
    n_logits = N_GROUPS + N_EXPERTS
    wr = jnp.concatenate([w_router_group[0], w_router_expert[0].reshape(d, N_EXPERTS),
                          jnp.zeros((d, LANES - n_logits), F32)], axis=1)
    wr_hi = wr.astype(BF16)
    wr_split = jnp.concatenate([wr_hi, (wr - wr_hi.astype(F32)).astype(BF16)], axis=1)
    br = jnp.concatenate([b_router_group[0], b_router_expert[0].reshape(N_EXPERTS),
                          jnp.zeros((LANES - n_logits,), F32)]).reshape(LANES, 1)
    meta_pad = jnp.concatenate([jnp.zeros((META_TILE - N_META, d), F32), meta_tokens.astype(F32)], axis=0)

    hq, hf, hi, hg, sq, sk, sv = _proj(x2d, w_in_b, 512)
    _, mf, mi, _, _, mk, mv = _proj(meta_pad, w_in_b, META_TILE)

    o_hg = _hgrn(hq, hf, hi, hg, mf, mi, lb, hg_norm_g[0].reshape(1, HG_WIDTH), batch, seq)
    o_sb = _stickbreak(sq, sk, sv, mk, mv, sb_norm_g[0].reshape(1, SB_WIDTH), batch, seq)

    hx, cls, rank, counts = _outproj(x2d, o_hg, o_sb, w_out_b, ln1_g[0].reshape(1, d), ln1_b[0].reshape(1, d),
                                     wr_split, br, 512)
    pos, tile_ea, tile_eb, tile_rows, zfrom, zto = _route(cls, rank, counts, n)
    xs = _scatter_tokens(pos, zfrom, zto, hx, tile_ea.shape[0] * MOE_TILE)
    ys = _moe(tile_ea, tile_eb, tile_rows, xs, w_exp_gate[0].astype(BF16), w_exp_up[0].astype(BF16),
              w_exp_down[0].astype(BF16), wr_split, br.reshape(1, LANES), ln2_g[0].reshape(1, d),
              ln2_b[0].reshape(1, d))
    return _gather_tokens(pos, ys).reshape(batch, seq, d)
```

```python
import functools

import jax
import jax.numpy as jnp
from jax import lax
from jax.experimental import pallas as pl
from jax.experimental.pallas import tpu as pltpu

F32 = jnp.float32
BF16 = jnp.bfloat16

D_MODEL = 1024
N_META = 16
HG_WIDTH = 512
HG_HEADS = 4
HG_DK = 128
SB_WIDTH = 512
SB_DH = 64
SB_PAIRS = 4
N_GROUPS = 4
EXPERTS_PER_GROUP = 4
N_EXPERTS = 16
D_EXPERT = 512
DEPTH = 1
ALPHA = (2 * DEPTH) ** 0.25
LN_EPS = 1e-5
RMS_EPS = 1e-6
N_SPLITS = 7
PROJ_DTYPES = (F32, F32, F32, F32, BF16, BF16, BF16)

N_PAIRS = 6
N_CLASSES = N_GROUPS * N_PAIRS
CLS_ROWS = 32
H_ROWS = D_MODEL // 128
MOE_TILE = 512
MOE_PARTS = 2
OUT_PARTS = 2
PERM_TOKENS = 2048
ZERO_TOKENS = 64
ISSUE_UNROLL = 8

LANES = 128
HG_CHUNK = 128
HG_STEP_CHUNKS = 8
SB_TILE = 128
META_TILE = 128
SKIP_LOG = -88.0
MASKED_SCORE = -1e30
VMEM_LIMIT = 56 * 1024 * 1024

NT_DIMS = (((1,), (1,)), ((), ()))


def _dot(a, b):
    return jnp.dot(a, b, preferred_element_type=F32)


def _dot_nt(a, b):
    return lax.dot_general(a, b, NT_DIMS, preferred_element_type=F32)


def _split3(x):
    hi = x.astype(BF16)
    r1 = x - hi.astype(F32)
    mid = r1.astype(BF16)
    lo = (r1 - mid.astype(F32)).astype(BF16)
    return hi, mid, lo


def _split2(x):
    hi = x.astype(BF16)
    return hi, (x - hi.astype(F32)).astype(BF16)


def _proj_kernel(x_ref, w_ref, *out_refs):
    xb = x_ref[...].astype(BF16)
    for j, o_ref in enumerate(out_refs):
        o_ref[...] = _dot(xb, w_ref[:, j * 512:(j + 1) * 512]).astype(o_ref.dtype)


def _proj(x2d, w_bf16, tm):
    n = x2d.shape[0]
    out_shape = [jax.ShapeDtypeStruct((n, 512), dt) for dt in PROJ_DTYPES]
    return pl.pallas_call(
        _proj_kernel,
        grid=(n // tm,),
        in_specs=[pl.BlockSpec((tm, D_MODEL), lambda i: (i, 0)),
                  pl.BlockSpec((D_MODEL, N_SPLITS * 512), lambda i: (0, 0))],
        out_specs=[pl.BlockSpec((tm, 512), lambda i: (i, 0)) for _ in range(N_SPLITS)],
        out_shape=out_shape,
        compiler_params=pltpu.CompilerParams(dimension_semantics=("parallel",),
                                             vmem_limit_bytes=VMEM_LIMIT),
        name="proj",
    )(x2d, w_bf16)


def _hgrn_chunk(zfs, zis, lbs, sts, tri, zqs=None, causal=None):
    n = len(zfs)
    c = zfs[0].shape[0]
    ks = [(1.0 - lbs[h]) * jax.nn.sigmoid(-zfs[h]) for h in range(n)]
    splits = [_split3(jnp.log(1.0 - k)) for k in ks]
    bcs = [_dot(tri, hi) + _dot(tri, mid) + _dot(tri, lo) for hi, mid, lo in splits]
    b_last = [bc[c - 1:c, :] for bc in bcs]
    os = None
    if zqs is not None:
        qs = [zq * jax.nn.sigmoid(zq) for zq in zqs]
        refs = [bc[c // 2 - 1:c // 2, :] for bc in bcs]
        qes = [(qs[h] * jnp.exp(bcs[h] - refs[h])).astype(BF16) for h in range(n)]
        kes = [(ks[h] * jnp.exp(refs[h] - bcs[h])).astype(BF16) for h in range(n)]
        qbs = [(qs[h] * jnp.exp(bcs[h])).astype(BF16) for h in range(n)]
        scs = [jnp.where(causal, _dot_nt(qes[h], kes[h]), 0.0).astype(BF16) for h in range(n)]
        os = [_dot(scs[h], zis[h].astype(BF16)) + _dot_nt(qbs[h], sts[h].astype(BF16)) for h in range(n)]
    kls = [(ks[h] * jnp.exp(b_last[h] - bcs[h])).astype(BF16) for h in range(n)]
    new_sts = [sts[h] * jnp.exp(b_last[h]) + _dot(zis[h].T.astype(BF16), kls[h]) for h in range(n)]
    return new_sts, os


def _hgrn_kernel(q_ref, f_ref, i_ref, g_ref, mf_ref, mi_ref, lb_ref, ng_ref, o_ref, st_ref):
    c = HG_CHUNK
    row = lax.broadcasted_iota(jnp.int32, (c, c), 0)
    col = lax.broadcasted_iota(jnp.int32, (c, c), 1)
    causal = col <= row
    tri = causal.astype(BF16)
    sls = [slice(h * HG_DK, (h + 1) * HG_DK) for h in range(HG_HEADS)]
    lbs = [lb_ref[:, sl] for sl in sls]

    @pl.when(pl.program_id(1) == 0)
    def _():
        zero = jnp.zeros((HG_DK, HG_DK), F32)
        sts, _ = _hgrn_chunk([mf_ref[:, sl] for sl in sls], [mi_ref[:, sl] for sl in sls], lbs,
                             [zero] * HG_HEADS, tri)
        for h in range(HG_HEADS):
            st_ref[h] = sts[h]

    sts = [st_ref[h] for h in range(HG_HEADS)]
    for j in range(HG_STEP_CHUNKS):
        rows = slice(j * c, (j + 1) * c)
        sts, os = _hgrn_chunk([f_ref[rows, sl] for sl in sls], [i_ref[rows, sl] for sl in sls], lbs, sts, tri,
                              zqs=[q_ref[rows, sl] for sl in sls], causal=causal)
        for h, sl in enumerate(sls):
            ms = jnp.mean(os[h] * os[h], axis=-1, keepdims=True)
            y = os[h] * lax.rsqrt(ms + RMS_EPS) * ng_ref[:, sl] * jax.nn.sigmoid(g_ref[rows, sl])
            o_ref[rows, sl] = y.astype(o_ref.dtype)
    for h in range(HG_HEADS):
        st_ref[h] = sts[h]


def _hgrn(hq, hf, hi, hg, mf, mi, lb, ng, batch, seq):
    step_rows = HG_CHUNK * HG_STEP_CHUNKS
    nc = seq // step_rows
    blk = pl.BlockSpec((step_rows, HG_WIDTH), lambda b, c: (b * nc + c, 0))
    meta = pl.BlockSpec((META_TILE, HG_WIDTH), lambda b, c: (0, 0))
    vec = pl.BlockSpec((1, HG_WIDTH), lambda b, c: (0, 0))
    return pl.pallas_call(
        _hgrn_kernel,
        grid=(batch, nc),
        in_specs=[blk, blk, blk, blk, meta, meta, vec, vec],
        out_specs=blk,
        out_shape=jax.ShapeDtypeStruct((batch * seq, HG_WIDTH), BF16),
        scratch_shapes=[pltpu.VMEM((HG_HEADS, HG_DK, HG_DK), F32)],
        compiler_params=pltpu.CompilerParams(dimension_semantics=("parallel", "arbitrary"),
                                             vmem_limit_bytes=VMEM_LIMIT),
        name="hgrn2",
    )(hq, hf, hi, hg, mf, mi, lb, ng)


def _sb_tiles(qs, tiles, rev, carries):
    n = len(qs)
    chains = [(j, p) for j in range(len(tiles)) for p in range(n)]
    zs = {c: _dot_nt(qs[c[1]], tiles[c[0]][0][c[1]]) for c in chains}
    ls_pos, lks = {}, {}
    for c in chains:
        z, mask = zs[c], tiles[c[0]][2]
        if mask is not None:
            z = jnp.where(mask, z, MASKED_SCORE)
        sp = jnp.log(1.0 + jnp.exp(-jnp.abs(z)))
        ls_pos[c] = jnp.minimum(z, 0.0) - sp
        lks[c] = ls_pos[c] - z
    css = {c: _dot(jnp.concatenate(_split2(lks[c]), axis=1), rev) for c in chains}
    sums = {c: jnp.sum(lks[c], axis=-1, keepdims=True) for c in chains}
    parts = [None] * n
    carries = list(carries)
    for j, p in chains:
        a = jnp.exp(ls_pos[j, p] + css[j, p] + carries[p])
        part = _dot(a.astype(BF16), tiles[j][1][p])
        parts[p] = part if parts[p] is None else parts[p] + part
        carries[p] = carries[p] + sums[j, p]
    return parts, carries


def _sb_kernel(q_ref, k_ref, v_ref, mk_ref, mv_ref, ng_ref, o_ref, acc_ref, car_ref):
    t = SB_TILE
    qi = pl.program_id(1)
    lane = lax.broadcasted_iota(jnp.int32, (1, LANES), 1)
    lo = lane < SB_DH
    row = lax.broadcasted_iota(jnp.int32, (2 * t, t), 0)
    col = lax.broadcasted_iota(jnp.int32, (2 * t, t), 1)
    rev = (lax.broadcasted_iota(jnp.int32, (t, t), 0)
           > lax.broadcasted_iota(jnp.int32, (t, t), 1)).astype(BF16)
    rev = jnp.concatenate([rev, rev], axis=0)
    diag_mask = col < jnp.where(row >= t, row - t, row)
    meta_mask = col >= META_TILE - N_META

    def pair(ref, s, p):
        return ref[pl.ds(s, t), p * LANES:(p + 1) * LANES]

    qs = []
    for p in range(SB_PAIRS):
        q = q_ref[:, p * LANES:(p + 1) * LANES] * (SB_DH ** -0.5)
        qs.append(jnp.concatenate([jnp.where(lo, q, 0.0), jnp.where(lo, 0.0, q)], axis=0).astype(BF16))

    def x_tile(s, mask):
        return ([pair(k_ref, s, p) for p in range(SB_PAIRS)], [pair(v_ref, s, p) for p in range(SB_PAIRS)], mask)

    def meta_pair(ref, p):
        return ref[:, p * LANES:(p + 1) * LANES]

    def alive_of(carries):
        m = carries[0]
        for c in carries[1:]:
            m = jnp.maximum(m, c)
        return jnp.max(m) > SKIP_LOG

    has_prev = jnp.full((t, LANES), qi, jnp.int32) > 0
    prev = pl.multiple_of(jnp.maximum(qi - 1, 0) * t, t)
    prev_tile = ([jnp.where(has_prev, pair(k_ref, prev, p), meta_pair(mk_ref, p)) for p in range(SB_PAIRS)],
                 [jnp.where(has_prev, pair(v_ref, prev, p), meta_pair(mv_ref, p)) for p in range(SB_PAIRS)],
                 jnp.logical_or(jnp.full((2 * t, t), qi, jnp.int32) > 0, meta_mask))
    zero = jnp.zeros((2 * t, 1), F32)
    accs, carries = _sb_tiles(qs, [x_tile(pl.multiple_of(qi * t, t), diag_mask), prev_tile], rev,
                              [zero] * SB_PAIRS)

    def cond(state):
        return jnp.logical_and(state[0] >= 0, state[1])

    def body(state):
        kt, _, accs, carries = state
        parts, carries = _sb_tiles(qs, [x_tile(pl.multiple_of(kt * t, t), None)], rev, carries)
        return kt - 1, alive_of(carries), [a + b for a, b in zip(accs, parts)], carries

    _, alive, accs, carries = lax.while_loop(cond, body, (qi - 2, alive_of(carries), accs, carries))
    for p in range(SB_PAIRS):
        acc_ref[p] = accs[p]
        car_ref[p] = carries[p]

    @pl.when(jnp.logical_and(alive, qi > 0))
    def _():
        meta_tile = ([meta_pair(mk_ref, p) for p in range(SB_PAIRS)],
                     [meta_pair(mv_ref, p) for p in range(SB_PAIRS)], meta_mask)
        parts, _ = _sb_tiles(qs, [meta_tile], rev, [car_ref[p] for p in range(SB_PAIRS)])
        for p in range(SB_PAIRS):
            acc_ref[p] += parts[p]

    for p in range(SB_PAIRS):
        sl = slice(p * LANES, (p + 1) * LANES)
        o = jnp.where(lo, acc_ref[p, :t, :], acc_ref[p, t:, :])
        sq = o * o
        ms_lo = jnp.sum(jnp.where(lo, sq, 0.0), axis=-1, keepdims=True) * (1.0 / SB_DH)
        ms_hi = jnp.sum(jnp.where(lo, 0.0, sq), axis=-1, keepdims=True) * (1.0 / SB_DH)
        inv = jnp.where(lo, lax.rsqrt(ms_lo + RMS_EPS), lax.rsqrt(ms_hi + RMS_EPS))
        o_ref[:, sl] = (o * inv * ng_ref[:, sl]).astype(o_ref.dtype)


def _stickbreak(sq, sk, sv, mk, mv, ng, batch, seq):
    nq = seq // SB_TILE
    qblk = pl.BlockSpec((SB_TILE, SB_WIDTH), lambda b, i: (b * nq + i, 0))
    kvblk = pl.BlockSpec((seq, SB_WIDTH), lambda b, i: (b, 0))
    mblk = pl.BlockSpec((META_TILE, SB_WIDTH), lambda b, i: (0, 0))
    gblk = pl.BlockSpec((1, SB_WIDTH), lambda b, i: (0, 0))
    return pl.pallas_call(
        _sb_kernel,
        grid=(batch, nq),
        in_specs=[qblk, kvblk, kvblk, mblk, mblk, gblk],
        out_specs=qblk,
        out_shape=jax.ShapeDtypeStruct((batch * seq, SB_WIDTH), BF16),
        scratch_shapes=[pltpu.VMEM((SB_PAIRS, 2 * SB_TILE, LANES), F32),
                        pltpu.VMEM((SB_PAIRS, 2 * SB_TILE, 1), F32)],
        compiler_params=pltpu.CompilerParams(dimension_semantics=("parallel", "arbitrary"),
                                             vmem_limit_bytes=VMEM_LIMIT),
        name="stickbreak",
    )(sq, sk, sv, mk, mv, ng)


def _layer_norm(x, g, b):
    mu = jnp.mean(x, axis=-1, keepdims=True)
    xc = x - mu
    var = jnp.mean(xc * xc, axis=-1, keepdims=True)
    return xc * lax.rsqrt(var + LN_EPS) * g + b


def _argmax_first(rows):
    best = rows[0]
    idx = jnp.zeros(best.shape, jnp.int32)
    for j in range(1, len(rows)):
        better = rows[j] > best
        idx = jnp.where(better, j, idx)
        best = jnp.where(better, rows[j], best)
    return best, idx


def _router_logits(h1, wr_ref):
    h_hi, h_mid = _split2(h1)
    both = _dot(h_hi, wr_ref[...])
    return both[:, :LANES] + both[:, LANES:] + _dot(h_mid, wr_ref[:, :LANES])


def _outproj_kernel(x_ref, ohg_ref, osb_ref, w_ref, g_ref, b_ref, wr_ref, br_ref,
                    hx_ref, cls_ref, rank_ref, cnt_ref, run_ref):
    tm = x_ref.shape[0]

    @pl.when(pl.program_id(0) == 0)
    def _():
        run_ref[...] = jnp.zeros(run_ref.shape, F32)

    rp = tm // OUT_PARTS
    half = D_MODEL // 2
    rows = [slice(m * rp, (m + 1) * rp) for m in range(OUT_PARTS)]
    mixed = [jnp.concatenate([ohg_ref[r, :], osb_ref[r, :]], axis=1) for r in rows]
    ys = [jnp.concatenate([_dot(mx, w_ref[:, :half]), _dot(mx, w_ref[:, half:])], axis=1) for mx in mixed]
    h1s = [_layer_norm(ALPHA * x_ref[r, :] + y, g_ref[...], b_ref[...]) for r, y in zip(rows, ys)]
    for m, h1 in enumerate(h1s):
        for s in range(H_ROWS):
            hx_ref[pl.ds(m * rp * H_ROWS + s, rp, stride=H_ROWS), :] = h1[:, s * LANES:(s + 1) * LANES]

    lt = jnp.concatenate([_router_logits(h1, wr_ref).T for h1 in h1s], axis=1) + br_ref[...]
    _, grp = _argmax_first([lt[g:g + 1, :] for g in range(N_GROUPS)])

    li = []
    for e in range(EXPERTS_PER_GROUP):
        acc = jnp.where(grp == 0, lt[N_GROUPS + e:N_GROUPS + e + 1, :], 0.0)
        for g in range(1, N_GROUPS):
            r = N_GROUPS + g * EXPERTS_PER_GROUP + e
            acc = acc + jnp.where(grp == g, lt[r:r + 1, :], 0.0)
        li.append(acc)
    v1, i1 = _argmax_first(li)
    neg = jnp.full_like(v1, -jnp.inf)
    _, i2 = _argmax_first([jnp.where(i1 == e, neg, li[e]) for e in range(EXPERTS_PER_GROUP)])

    e_lo = jnp.minimum(i1, i2)
    e_hi = jnp.maximum(i1, i2)
    pair = jnp.where(e_lo == 0, e_hi - 1, jnp.where(e_lo == 1, e_hi + 1, 5))
    cls = grp * N_PAIRS + pair

    onehot = lax.broadcasted_iota(jnp.int32, (CLS_ROWS, tm), 0) == cls
    oh = onehot.astype(BF16)
    earlier = (lax.broadcasted_iota(jnp.int32, (tm, tm), 0)
               < lax.broadcasted_iota(jnp.int32, (tm, tm), 1)).astype(BF16)
    run = run_ref[...]
    rank = jnp.sum(jnp.where(onehot, _dot(oh, earlier) + run, 0.0), axis=0, keepdims=True)
    cls_ref[...] = cls
    rank_ref[...] = rank.astype(jnp.int32)
    run = run + jnp.sum(oh.astype(F32), axis=1, keepdims=True)
    run_ref[...] = run
    cnt_ref[...] = jnp.broadcast_to(run, cnt_ref.shape)


def _outproj(x2d, ohg, osb, w_bf16, g, b, wr_split, br, tm):
    n = x2d.shape[0]
    const = lambda shape: pl.BlockSpec(shape, lambda i: (0, 0))
    return pl.pallas_call(
        _outproj_kernel,
        grid=(n // tm,),
        in_specs=[pl.BlockSpec((tm, D_MODEL), lambda i: (i, 0)),
                  pl.BlockSpec((tm, HG_WIDTH), lambda i: (i, 0)),
                  pl.BlockSpec((tm, SB_WIDTH), lambda i: (i, 0)),
                  const((D_MODEL, D_MODEL)), const((1, D_MODEL)), const((1, D_MODEL)),
                  const((D_MODEL, 2 * LANES)), const((LANES, 1))],
        out_specs=[pl.BlockSpec((tm * H_ROWS, LANES), lambda i: (i, 0)),
                   pl.BlockSpec((1, tm), lambda i: (0, i)),
                   pl.BlockSpec((1, tm), lambda i: (0, i)),
                   const((CLS_ROWS, LANES))],
        out_shape=[jax.ShapeDtypeStruct((n * H_ROWS, LANES), F32),
                   jax.ShapeDtypeStruct((1, n), jnp.int32),
                   jax.ShapeDtypeStruct((1, n), jnp.int32),
                   jax.ShapeDtypeStruct((CLS_ROWS, LANES), F32)],
        scratch_shapes=[pltpu.VMEM((CLS_ROWS, 1), F32)],
        compiler_params=pltpu.CompilerParams(dimension_semantics=("arbitrary",),
                                             vmem_limit_bytes=VMEM_LIMIT),
        name="outproj",
    )(x2d, ohg, osb, w_bf16, g, b, wr_split, br)


def _slab(ref, token, rows):
    return ref.at[pl.ds(pl.multiple_of(token * rows, rows), rows)]


def _scatter_kernel(pos_ref, zfrom_ref, zto_ref, src_ref, dst_ref, zero_ref, sem, zsem):
    base = pl.program_id(0) * PERM_TOKENS

    @pl.when(pl.program_id(0) == 0)
    def _():
        zero_ref[...] = jnp.zeros(zero_ref.shape, F32)

        def fill(k):
            return pltpu.make_async_copy(zero_ref, _slab(dst_ref, k, ZERO_TOKENS * H_ROWS), zsem)

        def each_chunk(do):
            def per_range(c, carry):
                def per_chunk(k, carry):
                    do(fill(k))
                    return carry
                return lax.fori_loop(zfrom_ref[c], zto_ref[c], per_chunk, carry)
            lax.fori_loop(0, zfrom_ref.shape[0], per_range, 0)

        each_chunk(lambda copy: copy.start())
        each_chunk(lambda copy: copy.wait())

    def issue(g, carry):
        for u in range(ISSUE_UNROLL):
            r = g * ISSUE_UNROLL + u
            pltpu.make_async_copy(_slab(src_ref, r, H_ROWS), _slab(dst_ref, pos_ref[base + r], H_ROWS),
                                  sem).start(priority=u % 2)
        return carry

    lax.fori_loop(0, PERM_TOKENS // ISSUE_UNROLL, issue, 0)
    pltpu.make_async_copy(src_ref, dst_ref.at[pl.ds(0, PERM_TOKENS * H_ROWS)], sem).wait()


def _scatter_tokens(pos, zfrom, zto, hx, sorted_tokens):
    n = pos.shape[0]
    return pl.pallas_call(
        _scatter_kernel,
        grid_spec=pltpu.PrefetchScalarGridSpec(
            num_scalar_prefetch=3,
            grid=(n // PERM_TOKENS,),
            in_specs=[pl.BlockSpec((PERM_TOKENS * H_ROWS, LANES), lambda i, pos, zf, zt: (i, 0))],
            out_specs=pl.BlockSpec(memory_space=pl.ANY),
            scratch_shapes=[pltpu.VMEM((ZERO_TOKENS * H_ROWS, LANES), F32),
                            pltpu.SemaphoreType.DMA(()), pltpu.SemaphoreType.DMA(())]),
        out_shape=jax.ShapeDtypeStruct((sorted_tokens * H_ROWS, LANES), F32),
        compiler_params=pltpu.CompilerParams(dimension_semantics=("arbitrary",),
                                             vmem_limit_bytes=VMEM_LIMIT),
        name="scatter_tokens",
    )(pos, zfrom, zto, hx)


def _gather_kernel(pos_ref, src_ref, o_ref, buf_ref, sem):
    base = pl.program_id(0) * PERM_TOKENS

    def issue(g, carry):
        for u in range(ISSUE_UNROLL):
            r = g * ISSUE_UNROLL + u
            pltpu.make_async_copy(_slab(src_ref, pos_ref[base + r], H_ROWS), _slab(buf_ref, r, H_ROWS),
                                  sem).start(priority=u % 2)
        return carry

    lax.fori_loop(0, PERM_TOKENS // ISSUE_UNROLL, issue, 0)
    pltpu.make_async_copy(src_ref.at[pl.ds(0, PERM_TOKENS * H_ROWS)], buf_ref, sem).wait()
    for s in range(H_ROWS):
        o_ref[:, s * LANES:(s + 1) * LANES] = buf_ref[pl.ds(s, PERM_TOKENS, stride=H_ROWS), :]


def _gather_tokens(pos, ys):
    n = pos.shape[0]
    return pl.pallas_call(
        _gather_kernel,
        grid_spec=pltpu.PrefetchScalarGridSpec(
            num_scalar_prefetch=1,
            grid=(n // PERM_TOKENS,),
            in_specs=[pl.BlockSpec(memory_space=pl.ANY)],
            out_specs=pl.BlockSpec((PERM_TOKENS, D_MODEL), lambda i, pos: (i, 0)),
            scratch_shapes=[pltpu.VMEM((PERM_TOKENS * H_ROWS, LANES), F32), pltpu.SemaphoreType.DMA(())]),
        out_shape=jax.ShapeDtypeStruct((n, D_MODEL), F32),
        compiler_params=pltpu.CompilerParams(dimension_semantics=("arbitrary",),
                                             vmem_limit_bytes=VMEM_LIMIT),
        name="gather_tokens",
    )(pos, ys)


def _moe_kernel(ea_ref, eb_ref, rows_ref, x_ref, w1a_ref, w3a_ref, w2a_ref, w1b_ref, w3b_ref, w2b_ref,
                wr_ref, br_ref, g_ref, b_ref, o_ref):
    i = pl.program_id(0)
    t = MOE_TILE

    @pl.when(rows_ref[i] > 0)
    def _():
        rp = t // MOE_PARTS
        lane = lax.broadcasted_iota(jnp.int32, (1, LANES), 1)
        pick = lambda lg, j: jnp.sum(jnp.where(lane == j, lg, 0.0), axis=-1, keepdims=True)
        parts = range(MOE_PARTS)
        h1s = [jnp.concatenate([x_ref[pl.ds(m * rp * H_ROWS + s, rp, stride=H_ROWS), :] for s in range(H_ROWS)],
                               axis=1) for m in parts]
        hbs = [h1.astype(BF16) for h1 in h1s]

        gates = []
        for h1 in h1s:
            lg = _router_logits(h1, wr_ref) + br_ref[...]
            l_grp = pick(lg, ea_ref[i] // EXPERTS_PER_GROUP)
            l_a, l_b = pick(lg, N_GROUPS + ea_ref[i]), pick(lg, N_GROUPS + eb_ref[i])
            p_grp = 1.0 / jnp.sum(jnp.where(lane < N_GROUPS, jnp.exp(lg - l_grp), 0.0), axis=-1, keepdims=True)
            gates.append((p_grp / (1.0 + jnp.exp(l_b - l_a)), p_grp / (1.0 + jnp.exp(l_a - l_b))))

        def hidden(hb, w1_ref, w3_ref, gate):
            a = _dot(hb, w1_ref[0])
            return (gate * (a * jax.nn.sigmoid(a)) * _dot(hb, w3_ref[0])).astype(BF16)

        hid_a = [hidden(hbs[m], w1a_ref, w3a_ref, gates[m][0]) for m in parts]
        hid_b = [hidden(hbs[m], w1b_ref, w3b_ref, gates[m][1]) for m in parts]
        ys = [_dot(hid_a[m], w2a_ref[0]) + _dot(hid_b[m], w2b_ref[0]) for m in parts]
        for m in parts:
            out = _layer_norm(ALPHA * h1s[m] + ys[m], g_ref[...], b_ref[...])
            for s in range(H_ROWS):
                o_ref[pl.ds(m * rp * H_ROWS + s, rp, stride=H_ROWS), :] = out[:, s * LANES:(s + 1) * LANES]

    @pl.when(rows_ref[i] == 0)
    def _():
        o_ref[...] = jnp.zeros(o_ref.shape, F32)


def _moe(tile_ea, tile_eb, tile_rows, xs, w1, w3, w2, wr, br_row, g, b):
    nt = tile_ea.shape[0]
    wa = lambda shape: pl.BlockSpec(shape, lambda i, ea, eb, r: (ea[i], 0, 0))
    wb = lambda shape: pl.BlockSpec(shape, lambda i, ea, eb, r: (eb[i], 0, 0))
    up, down = (1, D_MODEL, D_EXPERT), (1, D_EXPERT, D_MODEL)
    return pl.pallas_call(
        _moe_kernel,
        grid_spec=pltpu.PrefetchScalarGridSpec(
            num_scalar_prefetch=3,
            grid=(nt,),
            in_specs=[pl.BlockSpec((MOE_TILE * H_ROWS, LANES), lambda i, ea, eb, r: (i, 0)),
                      wa(up), wa(up), wa(down), wb(up), wb(up), wb(down),
                      pl.BlockSpec((D_MODEL, 2 * LANES), lambda i, ea, eb, r: (0, 0)),
                      pl.BlockSpec((1, LANES), lambda i, ea, eb, r: (0, 0)),
                      pl.BlockSpec((1, D_MODEL), lambda i, ea, eb, r: (0, 0)),
                      pl.BlockSpec((1, D_MODEL), lambda i, ea, eb, r: (0, 0))],
            out_specs=pl.BlockSpec((MOE_TILE * H_ROWS, LANES), lambda i, ea, eb, r: (i, 0))),
        out_shape=jax.ShapeDtypeStruct((nt * MOE_TILE * H_ROWS, LANES), F32),
        compiler_params=pltpu.CompilerParams(dimension_semantics=("arbitrary",),
                                             vmem_limit_bytes=VMEM_LIMIT),
        name="moe",
    )(tile_ea, tile_eb, tile_rows, xs, w1, w3, w2, w1, w3, w2, wr, br_row, g, b)


def _route(cls, rank, counts, n):
    cnt = counts[:N_CLASSES, 0].astype(jnp.int32)
    ntile = (cnt + MOE_TILE - 1) // MOE_TILE
    tile_end = jnp.cumsum(ntile)
    tile_start = tile_end - ntile
    pos = (tile_start * MOE_TILE)[cls.reshape(n)] + rank.reshape(n)
    tiles = jnp.arange(n // MOE_TILE + N_CLASSES, dtype=jnp.int32)
    tcls = jnp.minimum(jnp.sum((tiles[:, None] >= tile_end[None, :]).astype(jnp.int32), axis=1), N_CLASSES - 1)
    rows = jnp.clip(cnt[tcls] - (tiles - tile_start[tcls]) * MOE_TILE, 0, MOE_TILE)
    rows = jnp.where(tiles < tile_end[-1], rows, 0)
    pair_lo = jnp.array([0, 0, 0, 1, 1, 2], jnp.int32)
    pair_hi = jnp.array([1, 2, 3, 2, 3, 3], jnp.int32)
    grp, pair = tcls // N_PAIRS, tcls % N_PAIRS
    per_tile = MOE_TILE // ZERO_TOKENS
    zfrom = jnp.concatenate([(tile_start * MOE_TILE + cnt) // ZERO_TOKENS, tile_end[-1:] * per_tile])
    zto = jnp.concatenate([tile_end * per_tile, jnp.full((1,), tiles.shape[0] * per_tile, jnp.int32)])
    return (pos, grp * EXPERTS_PER_GROUP + pair_lo[pair], grp * EXPERTS_PER_GROUP + pair_hi[pair], rows,
            zfrom.astype(jnp.int32), zto.astype(jnp.int32))


def kernel(x, meta_tokens, w_in, hg_lower_bound, hg_norm_g, sb_norm_g, w_out, ln1_g, ln1_b,
           w_router_group, b_router_group, w_router_expert, b_router_expert,
           w_exp_gate, w_exp_up, w_exp_down, ln2_g, ln2_b):
    batch, seq, d = x.shape
    assert d == D_MODEL and seq % (HG_CHUNK * HG_STEP_CHUNKS) == 0 and seq % SB_TILE == 0
    assert w_in.shape[0] == DEPTH == 1
    n = batch * seq
    assert n % PERM_TOKENS == 0 and n % MOE_TILE == 0
    x2d = x.reshape(n, d)

    w_in_b = w_in[0].astype(BF16)
    w_out_b = w_out[0].astype(BF16)
    lb = jnp.cumsum(jax.nn.softmax(hg_lower_bound.astype(F32), axis=0), axis=0)[0].reshape(1, HG_WIDTH)
    n_logits = N_GROUPS + N_EXPERTS
    wr = jnp.concatenate([w_router_group[0], w_router_expert[0].reshape(d, N_EXPERTS),
                          jnp.zeros((d, LANES - n_logits), F32)], axis=1)
    wr_hi = wr.astype(BF16)
    wr_split = jnp.concatenate([wr_hi, (wr - wr_hi.astype(F32)).astype(BF16)], axis=1)
    br = jnp.concatenate([b_router_group[0], b_router_expert[0].reshape(N_EXPERTS),
                          jnp.zeros((LANES - n_logits,), F32)]).reshape(LANES, 1)
    meta_pad = jnp.concatenate([jnp.zeros((META_TILE - N_META, d), F32), meta_tokens.astype(F32)], axis=0)

    hq, hf, hi, hg, sq, sk, sv = _proj(x2d, w_in_b, 512)
    _, mf, mi, _, _, mk, mv = _proj(meta_pad, w_in_b, META_TILE)

    o_hg = _hgrn(hq, hf, hi, hg, mf, mi, lb, hg_norm_g[0].reshape(1, HG_WIDTH), batch, seq)
    o_sb = _stickbreak(sq, sk, sv, mk, mv, sb_norm_g[0].reshape(1, SB_WIDTH), batch, seq)

    hx, cls, rank, counts = _outproj(x2d, o_hg, o_sb, w_out_b, ln1_g[0].reshape(1, d), ln1_b[0].reshape(1, d),
                                     wr_split, br, 512)
    pos, tile_ea, tile_eb, tile_rows, zfrom, zto = _route(cls, rank, counts, n)
    xs = _scatter_tokens(pos, zfrom, zto, hx, tile_ea.shape[0] * MOE_TILE)
    ys = _moe(tile_ea, tile_eb, tile_rows, xs, w_exp_gate[0].astype(BF16), w_exp_up[0].astype(BF16),
              w_exp_down[0].astype(BF16), wr_split, br.reshape(1, LANES), ln2_g[0].reshape(1, d),
              ln2_b[0].reshape(1, d))
    return _gather_tokens(pos, ys).reshape(batch, seq, d)
```

```python
import functools

import jax
import jax.numpy as jnp
from jax import lax
from jax.experimental import pallas as pl
from jax.experimental.pallas import tpu as pltpu

F32 = jnp.float32
BF16 = jnp.bfloat16

D_MODEL = 1024
N_META = 16
HG_WIDTH = 512
HG_HEADS = 4
HG_DK = 128
SB_WIDTH = 512
SB_DH = 64
SB_PAIRS = 4
N_GROUPS = 4
EXPERTS_PER_GROUP = 4
N_EXPERTS = 16
D_EXPERT = 512
DEPTH = 1
ALPHA = (2 * DEPTH) ** 0.25
LN_EPS = 1e-5
RMS_EPS = 1e-6
N_SPLITS = 7
PROJ_DTYPES = (F32, F32, F32, F32, BF16, BF16, BF16)

N_PAIRS = 6
N_CLASSES = N_GROUPS * N_PAIRS
CLS_ROWS = 32
H_ROWS = D_MODEL // 128
MOE_TILE = 512
MOE_PARTS = 2
OUT_PARTS = 2
PERM_TOKENS = 2048
ZERO_TOKENS = 64
ISSUE_UNROLL = 8

LANES = 128
HG_CHUNK = 128
HG_STEP_CHUNKS = 4
HG_SEQS = 2
SB_TILE = 128
META_TILE = 128
SKIP_LOG = -88.0
MASKED_SCORE = -1e30
VMEM_LIMIT = 56 * 1024 * 1024

NT_DIMS = (((1,), (1,)), ((), ()))


def _dot(a, b):
    return jnp.dot(a, b, preferred_element_type=F32)


def _dot_nt(a, b):
    return lax.dot_general(a, b, NT_DIMS, preferred_element_type=F32)


def _split3(x):
    hi = x.astype(BF16)
    r1 = x - hi.astype(F32)
    mid = r1.astype(BF16)
    lo = (r1 - mid.astype(F32)).astype(BF16)
    return hi, mid, lo


def _split2(x):
    hi = x.astype(BF16)
    return hi, (x - hi.astype(F32)).astype(BF16)


def _proj_kernel(x_ref, w_ref, *out_refs):
    xb = x_ref[...].astype(BF16)
    for j, o_ref in enumerate(out_refs):
        o_ref[...] = _dot(xb, w_ref[:, j * 512:(j + 1) * 512]).astype(o_ref.dtype)


def _proj(x2d, w_bf16, tm):
    n = x2d.shape[0]
    out_shape = [jax.ShapeDtypeStruct((n, 512), dt) for dt in PROJ_DTYPES]
    return pl.pallas_call(
        _proj_kernel,
        grid=(n // tm,),
        in_specs=[pl.BlockSpec((tm, D_MODEL), lambda i: (i, 0)),
                  pl.BlockSpec((D_MODEL, N_SPLITS * 512), lambda i: (0, 0))],
        out_specs=[pl.BlockSpec((tm, 512), lambda i: (i, 0)) for _ in range(N_SPLITS)],
        out_shape=out_shape,
        compiler_params=pltpu.CompilerParams(dimension_semantics=("parallel",),
                                             vmem_limit_bytes=VMEM_LIMIT),
        name="proj",
    )(x2d, w_bf16)


def _hgrn_chunk(zfs, zis, lbs, sts, tri, zqs=None, causal=None):
    n = len(zfs)
    c = zfs[0].shape[0]
    ks = [(1.0 - lbs[h]) * jax.nn.sigmoid(-zfs[h]) for h in range(n)]
    splits = [_split3(jnp.log(1.0 - k)) for k in ks]
    bcs = [_dot(tri, hi) + _dot(tri, mid) + _dot(tri, lo) for hi, mid, lo in splits]
    b_last = [bc[c - 1:c, :] for bc in bcs]
    os = None
    if zqs is not None:
        qs = [zq * jax.nn.sigmoid(zq) for zq in zqs]
        refs = [bc[c // 2 - 1:c // 2, :] for bc in bcs]
        qes = [(qs[h] * jnp.exp(bcs[h] - refs[h])).astype(BF16) for h in range(n)]
        kes = [(ks[h] * jnp.exp(refs[h] - bcs[h])).astype(BF16) for h in range(n)]
        qbs = [(qs[h] * jnp.exp(bcs[h])).astype(BF16) for h in range(n)]
        scs = [jnp.where(causal, _dot_nt(qes[h], kes[h]), 0.0).astype(BF16) for h in range(n)]
        os = [_dot(scs[h], zis[h].astype(BF16)) + _dot_nt(qbs[h], sts[h].astype(BF16)) for h in range(n)]
    kls = [(ks[h] * jnp.exp(b_last[h] - bcs[h])).astype(BF16) for h in range(n)]
    new_sts = [sts[h] * jnp.exp(b_last[h]) + _dot(zis[h].T.astype(BF16), kls[h]) for h in range(n)]
    return new_sts, os


def _hgrn_kernel(q_ref, f_ref, i_ref, g_ref, mf_ref, mi_ref, lb_ref, ng_ref, o_ref, st_ref):
    c = HG_CHUNK
    row = lax.broadcasted_iota(jnp.int32, (c, c), 0)
    col = lax.broadcasted_iota(jnp.int32, (c, c), 1)
    causal = col <= row
    tri = causal.astype(BF16)
    sls = [slice(h * HG_DK, (h + 1) * HG_DK) for h in range(HG_HEADS)]
    chains = [(s, h) for s in range(HG_SEQS) for h in range(HG_HEADS)]
    lbs = [lb_ref[:, sls[h]] for _, h in chains]

    @pl.when(pl.program_id(1) == 0)
    def _():
        zero = jnp.zeros((HG_DK, HG_DK), F32)
        sts, _ = _hgrn_chunk([mf_ref[:, sl] for sl in sls], [mi_ref[:, sl] for sl in sls], lbs[:HG_HEADS],
                             [zero] * HG_HEADS, tri)
        for k, (_, h) in enumerate(chains):
            st_ref[k] = sts[h]

    sts = [st_ref[k] for k in range(len(chains))]
    for j in range(HG_STEP_CHUNKS):
        rows = slice(j * c, (j + 1) * c)
        sts, os = _hgrn_chunk([f_ref[s, rows, sls[h]] for s, h in chains],
                              [i_ref[s, rows, sls[h]] for s, h in chains], lbs, sts, tri,
                              zqs=[q_ref[s, rows, sls[h]] for s, h in chains], causal=causal)
        for k, (s, h) in enumerate(chains):
            ms = jnp.mean(os[k] * os[k], axis=-1, keepdims=True)
            y = os[k] * lax.rsqrt(ms + RMS_EPS) * ng_ref[:, sls[h]] * jax.nn.sigmoid(g_ref[s, rows, sls[h]])
            o_ref[s, rows, sls[h]] = y.astype(o_ref.dtype)
    for k in range(len(chains)):
        st_ref[k] = sts[k]


def _hgrn(hq, hf, hi, hg, mf, mi, lb, ng, batch, seq):
    step_rows = HG_CHUNK * HG_STEP_CHUNKS
    blk = pl.BlockSpec((HG_SEQS, step_rows, HG_WIDTH), lambda b, c: (b, c, 0))
    meta = pl.BlockSpec((META_TILE, HG_WIDTH), lambda b, c: (0, 0))
    vec = pl.BlockSpec((1, HG_WIDTH), lambda b, c: (0, 0))
    per_seq = lambda a: a.reshape(batch, seq, HG_WIDTH)
    return pl.pallas_call(
        _hgrn_kernel,
        grid=(batch // HG_SEQS, seq // step_rows),
        in_specs=[blk, blk, blk, blk, meta, meta, vec, vec],
        out_specs=blk,
        out_shape=jax.ShapeDtypeStruct((batch, seq, HG_WIDTH), BF16),
        scratch_shapes=[pltpu.VMEM((HG_SEQS * HG_HEADS, HG_DK, HG_DK), F32)],
        compiler_params=pltpu.CompilerParams(dimension_semantics=("parallel", "arbitrary"),
                                             vmem_limit_bytes=VMEM_LIMIT),
        name="hgrn2",
    )(per_seq(hq), per_seq(hf), per_seq(hi), per_seq(hg), mf, mi, lb, ng).reshape(batch * seq, HG_WIDTH)


def _sb_tiles(qs, tiles, rev, carries):
    n = len(qs)
    chains = [(j, p) for j in range(len(tiles)) for p in range(n)]
    zs = {c: _dot_nt(qs[c[1]], tiles[c[0]][0][c[1]]) for c in chains}
    ls_pos, lks = {}, {}
    for c in chains:
        z, mask = zs[c], tiles[c[0]][2]
        if mask is not None:
            z = jnp.where(mask, z, MASKED_SCORE)
        sp = jnp.log(1.0 + jnp.exp(-jnp.abs(z)))
        ls_pos[c] = jnp.minimum(z, 0.0) - sp
        lks[c] = ls_pos[c] - z
    css = {c: _dot(jnp.concatenate(_split2(lks[c]), axis=1), rev) for c in chains}
    sums = {c: jnp.sum(lks[c], axis=-1, keepdims=True) for c in chains}
    parts = [None] * n
    carries = list(carries)
    for j, p in chains:
        a = jnp.exp(ls_pos[j, p] + css[j, p] + carries[p])
        part = _dot(a.astype(BF16), tiles[j][1][p])
        parts[p] = part if parts[p] is None else parts[p] + part
        carries[p] = carries[p] + sums[j, p]
    return parts, carries


def _sb_kernel(q_ref, k_ref, v_ref, mk_ref, mv_ref, ng_ref, o_ref, acc_ref, car_ref):
    t = SB_TILE
    qi = pl.program_id(1)
    lane = lax.broadcasted_iota(jnp.int32, (1, LANES), 1)
    lo = lane < SB_DH
    row = lax.broadcasted_iota(jnp.int32, (2 * t, t), 0)
    col = lax.broadcasted_iota(jnp.int32, (2 * t, t), 1)
    rev = (lax.broadcasted_iota(jnp.int32, (t, t), 0)
           > lax.broadcasted_iota(jnp.int32, (t, t), 1)).astype(BF16)
    rev = jnp.concatenate([rev, rev], axis=0)
    diag_mask = col < jnp.where(row >= t, row - t, row)
    meta_mask = col >= META_TILE - N_META

    def pair(ref, s, p):
        return ref[pl.ds(s, t), p * LANES:(p + 1) * LANES]

    qs = []
    for p in range(SB_PAIRS):
        q = q_ref[:, p * LANES:(p + 1) * LANES] * (SB_DH ** -0.5)
        qs.append(jnp.concatenate([jnp.where(lo, q, 0.0), jnp.where(lo, 0.0, q)], axis=0).astype(BF16))

    def x_tile(s, mask):
        return ([pair(k_ref, s, p) for p in range(SB_PAIRS)], [pair(v_ref, s, p) for p in range(SB_PAIRS)], mask)

    def meta_pair(ref, p):
        return ref[:, p * LANES:(p + 1) * LANES]

    def alive_of(carries):
        m = carries[0]
        for c in carries[1:]:
            m = jnp.maximum(m, c)
        return jnp.max(m) > SKIP_LOG

    has_prev = jnp.full((t, LANES), qi, jnp.int32) > 0
    prev = pl.multiple_of(jnp.maximum(qi - 1, 0) * t, t)
    prev_tile = ([jnp.where(has_prev, pair(k_ref, prev, p), meta_pair(mk_ref, p)) for p in range(SB_PAIRS)],
                 [jnp.where(has_prev, pair(v_ref, prev, p), meta_pair(mv_ref, p)) for p in range(SB_PAIRS)],
                 jnp.logical_or(jnp.full((2 * t, t), qi, jnp.int32) > 0, meta_mask))
    zero = jnp.zeros((2 * t, 1), F32)
    accs, carries = _sb_tiles(qs, [x_tile(pl.multiple_of(qi * t, t), diag_mask), prev_tile], rev,
                              [zero] * SB_PAIRS)

    def cond(state):
        return jnp.logical_and(state[0] >= 0, state[1])

    def body(state):
        kt, _, accs, carries = state
        parts, carries = _sb_tiles(qs, [x_tile(pl.multiple_of(kt * t, t), None)], rev, carries)
        return kt - 1, alive_of(carries), [a + b for a, b in zip(accs, parts)], carries

    _, alive, accs, carries = lax.while_loop(cond, body, (qi - 2, alive_of(carries), accs, carries))
    for p in range(SB_PAIRS):
        acc_ref[p] = accs[p]
        car_ref[p] = carries[p]

    @pl.when(jnp.logical_and(alive, qi > 0))
    def _():
        meta_tile = ([meta_pair(mk_ref, p) for p in range(SB_PAIRS)],
                     [meta_pair(mv_ref, p) for p in range(SB_PAIRS)], meta_mask)
        parts, _ = _sb_tiles(qs, [meta_tile], rev, [car_ref[p] for p in range(SB_PAIRS)])
        for p in range(SB_PAIRS):
            acc_ref[p] += parts[p]

    for p in range(SB_PAIRS):
        sl = slice(p * LANES, (p + 1) * LANES)
        o = jnp.where(lo, acc_ref[p, :t, :], acc_ref[p, t:, :])
        sq = o * o
        ms_lo = jnp.sum(jnp.where(lo, sq, 0.0), axis=-1, keepdims=True) * (1.0 / SB_DH)
        ms_hi = jnp.sum(jnp.where(lo, 0.0, sq), axis=-1, keepdims=True) * (1.0 / SB_DH)
        inv = jnp.where(lo, lax.rsqrt(ms_lo + RMS_EPS), lax.rsqrt(ms_hi + RMS_EPS))
        o_ref[:, sl] = (o * inv * ng_ref[:, sl]).astype(o_ref.dtype)


def _stickbreak(sq, sk, sv, mk, mv, ng, batch, seq):
    nq = seq // SB_TILE
    qblk = pl.BlockSpec((SB_TILE, SB_WIDTH), lambda b, i: (b * nq + i, 0))
    kvblk = pl.BlockSpec((seq, SB_WIDTH), lambda b, i: (b, 0))
    mblk = pl.BlockSpec((META_TILE, SB_WIDTH), lambda b, i: (0, 0))
    gblk = pl.BlockSpec((1, SB_WIDTH), lambda b, i: (0, 0))
    return pl.pallas_call(
        _sb_kernel,
        grid=(batch, nq),
        in_specs=[qblk, kvblk, kvblk, mblk, mblk, gblk],
        out_specs=qblk,
        out_shape=jax.ShapeDtypeStruct((batch * seq, SB_WIDTH), BF16),
        scratch_shapes=[pltpu.VMEM((SB_PAIRS, 2 * SB_TILE, LANES), F32),
                        pltpu.VMEM((SB_PAIRS, 2 * SB_TILE, 1), F32)],
        compiler_params=pltpu.CompilerParams(dimension_semantics=("parallel", "arbitrary"),
                                             vmem_limit_bytes=VMEM_LIMIT),
        name="stickbreak",
    )(sq, sk, sv, mk, mv, ng)


def _layer_norm(x, g, b):
    mu = jnp.mean(x, axis=-1, keepdims=True)
    xc = x - mu
    var = jnp.mean(xc * xc, axis=-1, keepdims=True)
    return xc * lax.rsqrt(var + LN_EPS) * g + b


def _argmax_first(rows):
    best = rows[0]
    idx = jnp.zeros(best.shape, jnp.int32)
    for j in range(1, len(rows)):
        better = rows[j] > best
        idx = jnp.where(better, j, idx)
        best = jnp.where(better, rows[j], best)
    return best, idx


def _router_logits(h1, wr_ref):
    h_hi, h_mid = _split2(h1)
    both = _dot(h_hi, wr_ref[...])
    return both[:, :LANES] + both[:, LANES:] + _dot(h_mid, wr_ref[:, :LANES])


def _outproj_kernel(x_ref, ohg_ref, osb_ref, w_ref, g_ref, b_ref, wr_ref, br_ref,
                    hx_ref, cls_ref, rank_ref, cnt_ref, run_ref):
    tm = x_ref.shape[0]

    @pl.when(pl.program_id(0) == 0)
    def _():
        run_ref[...] = jnp.zeros(run_ref.shape, F32)

    rp = tm // OUT_PARTS
    half = D_MODEL // 2
    rows = [slice(m * rp, (m + 1) * rp) for m in range(OUT_PARTS)]
    mixed = [jnp.concatenate([ohg_ref[r, :], osb_ref[r, :]], axis=1) for r in rows]
    ys = [jnp.concatenate([_dot(mx, w_ref[:, :half]), _dot(mx, w_ref[:, half:])], axis=1) for mx in mixed]
    h1s = [_layer_norm(ALPHA * x_ref[r, :] + y, g_ref[...], b_ref[...]) for r, y in zip(rows, ys)]
    for m, h1 in enumerate(h1s):
        for s in range(H_ROWS):
            hx_ref[pl.ds(m * rp * H_ROWS + s, rp, stride=H_ROWS), :] = h1[:, s * LANES:(s + 1) * LANES]

    lt = jnp.concatenate([_router_logits(h1, wr_ref).T for h1 in h1s], axis=1) + br_ref[...]
    _, grp = _argmax_first([lt[g:g + 1, :] for g in range(N_GROUPS)])

    li = []
    for e in range(EXPERTS_PER_GROUP):
        acc = jnp.where(grp == 0, lt[N_GROUPS + e:N_GROUPS + e + 1, :], 0.0)
        for g in range(1, N_GROUPS):
            r = N_GROUPS + g * EXPERTS_PER_GROUP + e
            acc = acc + jnp.where(grp == g, lt[r:r + 1, :], 0.0)
        li.append(acc)
    v1, i1 = _argmax_first(li)
    neg = jnp.full_like(v1, -jnp.inf)
    _, i2 = _argmax_first([jnp.where(i1 == e, neg, li[e]) for e in range(EXPERTS_PER_GROUP)])

    e_lo = jnp.minimum(i1, i2)
    e_hi = jnp.maximum(i1, i2)
    pair = jnp.where(e_lo == 0, e_hi - 1, jnp.where(e_lo == 1, e_hi + 1, 5))
    cls = grp * N_PAIRS + pair

    onehot = lax.broadcasted_iota(jnp.int32, (CLS_ROWS, tm), 0) == cls
    oh = onehot.astype(BF16)
    earlier = (lax.broadcasted_iota(jnp.int32, (tm, tm), 0)
               < lax.broadcasted_iota(jnp.int32, (tm, tm), 1)).astype(BF16)
    run = run_ref[...]
    rank = jnp.sum(jnp.where(onehot, _dot(oh, earlier) + run, 0.0), axis=0, keepdims=True)
    cls_ref[...] = cls
    rank_ref[...] = rank.astype(jnp.int32)
    run = run + jnp.sum(oh.astype(F32), axis=1, keepdims=True)
    run_ref[...] = run
    cnt_ref[...] = jnp.broadcast_to(run, cnt_ref.shape)


def _outproj(x2d, ohg, osb, w_bf16, g, b, wr_split, br, tm):
    n = x2d.shape[0]
    const = lambda shape: pl.BlockSpec(shape, lambda i: (0, 0))
    return pl.pallas_call(
        _outproj_kernel,
        grid=(n // tm,),
        in_specs=[pl.BlockSpec((tm, D_MODEL), lambda i: (i, 0)),
                  pl.BlockSpec((tm, HG_WIDTH), lambda i: (i, 0)),
                  pl.BlockSpec((tm, SB_WIDTH), lambda i: (i, 0)),
                  const((D_MODEL, D_MODEL)), const((1, D_MODEL)), const((1, D_MODEL)),
                  const((D_MODEL, 2 * LANES)), const((LANES, 1))],
        out_specs=[pl.BlockSpec((tm * H_ROWS, LANES), lambda i: (i, 0)),
                   pl.BlockSpec((1, tm), lambda i: (0, i)),
                   pl.BlockSpec((1, tm), lambda i: (0, i)),
                   const((CLS_ROWS, LANES))],
        out_shape=[jax.ShapeDtypeStruct((n * H_ROWS, LANES), F32),
                   jax.ShapeDtypeStruct((1, n), jnp.int32),
                   jax.ShapeDtypeStruct((1, n), jnp.int32),
                   jax.ShapeDtypeStruct((CLS_ROWS, LANES), F32)],
        scratch_shapes=[pltpu.VMEM((CLS_ROWS, 1), F32)],
        compiler_params=pltpu.CompilerParams(dimension_semantics=("arbitrary",),
                                             vmem_limit_bytes=VMEM_LIMIT),
        name="outproj",
    )(x2d, ohg, osb, w_bf16, g, b, wr_split, br)


def _slab(ref, token, rows):
    return ref.at[pl.ds(pl.multiple_of(token * rows, rows), rows)]


def _scatter_kernel(pos_ref, zfrom_ref, zto_ref, src_ref, dst_ref, zero_ref, sem, zsem):
    base = pl.program_id(0) * PERM_TOKENS

    @pl.when(pl.program_id(0) == 0)
    def _():
        zero_ref[...] = jnp.zeros(zero_ref.shape, F32)

        def fill(k):
            return pltpu.make_async_copy(zero_ref, _slab(dst_ref, k, ZERO_TOKENS * H_ROWS), zsem)

        def each_chunk(do):
            def per_range(c, carry):
                def per_chunk(k, carry):
                    do(fill(k))
                    return carry
                return lax.fori_loop(zfrom_ref[c], zto_ref[c], per_chunk, carry)
            lax.fori_loop(0, zfrom_ref.shape[0], per_range, 0)

        each_chunk(lambda copy: copy.start())
        each_chunk(lambda copy: copy.wait())

    def issue(g, carry):
        for u in range(ISSUE_UNROLL):
            r = g * ISSUE_UNROLL + u
            pltpu.make_async_copy(_slab(src_ref, r, H_ROWS), _slab(dst_ref, pos_ref[base + r], H_ROWS),
                                  sem).start(priority=u % 2)
        return carry

    lax.fori_loop(0, PERM_TOKENS // ISSUE_UNROLL, issue, 0)
    pltpu.make_async_copy(src_ref, dst_ref.at[pl.ds(0, PERM_TOKENS * H_ROWS)], sem).wait()


def _scatter_tokens(pos, zfrom, zto, hx, sorted_tokens):
    n = pos.shape[0]
    return pl.pallas_call(
        _scatter_kernel,
        grid_spec=pltpu.PrefetchScalarGridSpec(
            num_scalar_prefetch=3,
            grid=(n // PERM_TOKENS,),
            in_specs=[pl.BlockSpec((PERM_TOKENS * H_ROWS, LANES), lambda i, pos, zf, zt: (i, 0))],
            out_specs=pl.BlockSpec(memory_space=pl.ANY),
            scratch_shapes=[pltpu.VMEM((ZERO_TOKENS * H_ROWS, LANES), F32),
                            pltpu.SemaphoreType.DMA(()), pltpu.SemaphoreType.DMA(())]),
        out_shape=jax.ShapeDtypeStruct((sorted_tokens * H_ROWS, LANES), F32),
        compiler_params=pltpu.CompilerParams(dimension_semantics=("arbitrary",),
                                             vmem_limit_bytes=VMEM_LIMIT),
        name="scatter_tokens",
    )(pos, zfrom, zto, hx)


def _gather_kernel(pos_ref, src_ref, o_ref, buf_ref, sem):
    i = pl.program_id(0)
    last = pl.num_programs(0) - 1
    slot = i % 2
    groups = PERM_TOKENS // ISSUE_UNROLL

    def issue(g):
        for u in range(ISSUE_UNROLL):
            r = g * ISSUE_UNROLL + u
            pltpu.make_async_copy(_slab(src_ref, pos_ref[i * PERM_TOKENS + r], H_ROWS),
                                  _slab(buf_ref.at[slot], r, H_ROWS), sem.at[slot]).start(priority=u % 2)

    def relayout(g):
        rows = pl.ds(pl.multiple_of(g * ISSUE_UNROLL, ISSUE_UNROLL), ISSUE_UNROLL)
        for s in range(H_ROWS):
            o_ref[rows, s * LANES:(s + 1) * LANES] = buf_ref[
                1 - slot, pl.ds(g * ISSUE_UNROLL * H_ROWS + s, ISSUE_UNROLL, stride=H_ROWS), :]

    def wait_previous():
        pltpu.make_async_copy(src_ref.at[pl.ds(0, PERM_TOKENS * H_ROWS)], buf_ref.at[1 - slot],
                              sem.at[1 - slot]).wait()

    def loop(body):
        def step(g, carry):
            body(g)
            return carry
        lax.fori_loop(0, groups, step, 0)

    @pl.when(i == 0)
    def _():
        loop(issue)

    @pl.when(jnp.logical_and(i > 0, i < last))
    def _():
        wait_previous()
        loop(lambda g: (issue(g), relayout(g)))

    @pl.when(i == last)
    def _():
        wait_previous()
        loop(relayout)


def _gather_tokens(pos, ys):
    n = pos.shape[0]
    return pl.pallas_call(
        _gather_kernel,
        grid_spec=pltpu.PrefetchScalarGridSpec(
            num_scalar_prefetch=1,
            grid=(n // PERM_TOKENS + 1,),
            in_specs=[pl.BlockSpec(memory_space=pl.ANY)],
            out_specs=pl.BlockSpec((PERM_TOKENS, D_MODEL), lambda i, pos: (jnp.maximum(i - 1, 0), 0)),
            scratch_shapes=[pltpu.VMEM((2, PERM_TOKENS * H_ROWS, LANES), F32), pltpu.SemaphoreType.DMA((2,))]),
        out_shape=jax.ShapeDtypeStruct((n, D_MODEL), F32),
        compiler_params=pltpu.CompilerParams(dimension_semantics=("arbitrary",),
                                             vmem_limit_bytes=VMEM_LIMIT),
        name="gather_tokens",
    )(pos, ys)


def _moe_kernel(ea_ref, eb_ref, rows_ref, x_ref, w1a_ref, w3a_ref, w2a_ref, w1b_ref, w3b_ref, w2b_ref,
                wr_ref, br_ref, g_ref, b_ref, o_ref):
    i = pl.program_id(0)
    t = MOE_TILE

    @pl.when(rows_ref[i] > 0)
    def _():
        rp = t // MOE_PARTS
        lane = lax.broadcasted_iota(jnp.int32, (1, LANES), 1)
        pick = lambda lg, j: jnp.sum(jnp.where(lane == j, lg, 0.0), axis=-1, keepdims=True)
        parts = range(MOE_PARTS)
        h1s = [jnp.concatenate([x_ref[pl.ds(m * rp * H_ROWS + s, rp, stride=H_ROWS), :] for s in range(H_ROWS)],
                               axis=1) for m in parts]
        hbs = [h1.astype(BF16) for h1 in h1s]

        gates = []
        for h1 in h1s:
            lg = _router_logits(h1, wr_ref) + br_ref[...]
            l_grp = pick(lg, ea_ref[i] // EXPERTS_PER_GROUP)
            l_a, l_b = pick(lg, N_GROUPS + ea_ref[i]), pick(lg, N_GROUPS + eb_ref[i])
            p_grp = 1.0 / jnp.sum(jnp.where(lane < N_GROUPS, jnp.exp(lg - l_grp), 0.0), axis=-1, keepdims=True)
            gates.append((p_grp / (1.0 + jnp.exp(l_b - l_a)), p_grp / (1.0 + jnp.exp(l_a - l_b))))

        def hidden(hb, w1_ref, w3_ref, gate):
            a = _dot(hb, w1_ref[0])
            return (gate * (a * jax.nn.sigmoid(a)) * _dot(hb, w3_ref[0])).astype(BF16)

        hid_a = [hidden(hbs[m], w1a_ref, w3a_ref, gates[m][0]) for m in parts]
        hid_b = [hidden(hbs[m], w1b_ref, w3b_ref, gates[m][1]) for m in parts]
        ys = [_dot(hid_a[m], w2a_ref[0]) + _dot(hid_b[m], w2b_ref[0]) for m in parts]
        for m in parts:
            out = _layer_norm(ALPHA * h1s[m] + ys[m], g_ref[...], b_ref[...])
            for s in range(H_ROWS):
                o_ref[pl.ds(m * rp * H_ROWS + s, rp, stride=H_ROWS), :] = out[:, s * LANES:(s + 1) * LANES]

    @pl.when(rows_ref[i] == 0)
    def _():
        o_ref[...] = jnp.zeros(o_ref.shape, F32)


def _moe(tile_ea, tile_eb, tile_rows, xs, w1, w3, w2, wr, br_row, g, b):
    nt = tile_ea.shape[0]
    wa = lambda shape: pl.BlockSpec(shape, lambda i, ea, eb, r: (ea[i], 0, 0))
    wb = lambda shape: pl.BlockSpec(shape, lambda i, ea, eb, r: (eb[i], 0, 0))
    up, down = (1, D_MODEL, D_EXPERT), (1, D_EXPERT, D_MODEL)
    return pl.pallas_call(
        _moe_kernel,
        grid_spec=pltpu.PrefetchScalarGridSpec(
            num_scalar_prefetch=3,
            grid=(nt,),
            in_specs=[pl.BlockSpec((MOE_TILE * H_ROWS, LANES), lambda i, ea, eb, r: (i, 0)),
                      wa(up), wa(up), wa(down), wb(up), wb(up), wb(down),
                      pl.BlockSpec((D_MODEL, 2 * LANES), lambda i, ea, eb, r: (0, 0)),
                      pl.BlockSpec((1, LANES), lambda i, ea, eb, r: (0, 0)),
                      pl.BlockSpec((1, D_MODEL), lambda i, ea, eb, r: (0, 0)),
                      pl.BlockSpec((1, D_MODEL), lambda i, ea, eb, r: (0, 0))],
            out_specs=pl.BlockSpec((MOE_TILE * H_ROWS, LANES), lambda i, ea, eb, r: (i, 0))),
        out_shape=jax.ShapeDtypeStruct((nt * MOE_TILE * H_ROWS, LANES), F32),
        compiler_params=pltpu.CompilerParams(dimension_semantics=("arbitrary",),
                                             vmem_limit_bytes=VMEM_LIMIT),
        name="moe",
    )(tile_ea, tile_eb, tile_rows, xs, w1, w3, w2, w1, w3, w2, wr, br_row, g, b)


def _route(cls, rank, counts, n):
    cnt = counts[:N_CLASSES, 0].astype(jnp.int32)
    ntile = (cnt + MOE_TILE - 1) // MOE_TILE
    tile_end = jnp.cumsum(ntile)
    tile_start = tile_end - ntile
    pos = (tile_start * MOE_TILE)[cls.reshape(n)] + rank.reshape(n)
    tiles = jnp.arange(n // MOE_TILE + N_CLASSES, dtype=jnp.int32)
    tcls = jnp.minimum(jnp.sum((tiles[:, None] >= tile_end[None, :]).astype(jnp.int32), axis=1), N_CLASSES - 1)
    rows = jnp.clip(cnt[tcls] - (tiles - tile_start[tcls]) * MOE_TILE, 0, MOE_TILE)
    rows = jnp.where(tiles < tile_end[-1], rows, 0)
    pair_lo = jnp.array([0, 0, 0, 1, 1, 2], jnp.int32)
    pair_hi = jnp.array([1, 2, 3, 2, 3, 3], jnp.int32)
    grp, pair = tcls // N_PAIRS, tcls % N_PAIRS
    per_tile = MOE_TILE // ZERO_TOKENS
    zfrom = jnp.concatenate([(tile_start * MOE_TILE + cnt) // ZERO_TOKENS, tile_end[-1:] * per_tile])
    zto = jnp.concatenate([tile_end * per_tile, jnp.full((1,), tiles.shape[0] * per_tile, jnp.int32)])
    return (pos, grp * EXPERTS_PER_GROUP + pair_lo[pair], grp * EXPERTS_PER_GROUP + pair_hi[pair], rows,
            zfrom.astype(jnp.int32), zto.astype(jnp.int32))


def kernel(x, meta_tokens, w_in, hg_lower_bound, hg_norm_g, sb_norm_g, w_out, ln1_g, ln1_b,
           w_router_group, b_router_group, w_router_expert, b_router_expert,
           w_exp_gate, w_exp_up, w_exp_down, ln2_g, ln2_b):
    batch, seq, d = x.shape
    assert d == D_MODEL and seq % (HG_CHUNK * HG_STEP_CHUNKS) == 0 and seq % SB_TILE == 0
    assert w_in.shape[0] == DEPTH == 1 and batch % HG_SEQS == 0
    n = batch * seq
    assert n % PERM_TOKENS == 0 and n % MOE_TILE == 0
    x2d = x.reshape(n, d)

    w_in_b = w_in[0].astype(BF16)
    w_out_b = w_out[0].astype(BF16)
    lb = jnp.cumsum(jax.nn.softmax(hg_lower_bound.astype(F32), axis=0), axis=0)[0].reshape(1, HG_WIDTH)
    n_logits = N_GROUPS + N_EXPERTS
    wr = jnp.concatenate([w_router_group[0], w_router_expert[0].reshape(d, N_EXPERTS),
                          jnp.zeros((d, LANES - n_logits), F32)], axis=1)
    wr_hi = wr.astype(BF16)
    wr_split = jnp.concatenate([wr_hi, (wr - wr_hi.astype(F32)).astype(BF16)], axis=1)
    br = jnp.concatenate([b_router_group[0], b_router_expert[0].reshape(N_EXPERTS),
                          jnp.zeros((LANES - n_logits,), F32)]).reshape(LANES, 1)
    meta_pad = jnp.concatenate([jnp.zeros((META_TILE - N_META, d), F32), meta_tokens.astype(F32)], axis=0)

    hq, hf, hi, hg, sq, sk, sv = _proj(x2d, w_in_b, 512)
    _, mf, mi, _, _, mk, mv = _proj(meta_pad, w_in_b, META_TILE)

    o_hg = _hgrn(hq, hf, hi, hg, mf, mi, lb, hg_norm_g[0].reshape(1, HG_WIDTH), batch, seq)
    o_sb = _stickbreak(sq, sk, sv, mk, mv, sb_norm_g[0].reshape(1, SB_WIDTH), batch, seq)

    hx, cls, rank, counts = _outproj(x2d, o_hg, o_sb, w_out_b, ln1_g[0].reshape(1, d), ln1_b[0].reshape(1, d),
                                     wr_split, br, 512)
    pos, tile_ea, tile_eb, tile_rows, zfrom, zto = _route(cls, rank, counts, n)
    xs = _scatter_tokens(pos, zfrom, zto, hx, tile_ea.shape[0] * MOE_TILE)
    ys = _moe(tile_ea, tile_eb, tile_rows, xs, w_exp_gate[0].astype(BF16), w_exp_up[0].astype(BF16),
              w_exp_down[0].astype(BF16), wr_split, br.reshape(1, LANES), ln2_g[0].reshape(1, d),
              ln2_b[0].reshape(1, d))
    return _gather_tokens(pos, ys).reshape(batch, seq, d)
```

```python
import functools

import jax
import jax.numpy as jnp
from jax import lax
from jax.experimental import pallas as pl
from jax.experimental.pallas import tpu as pltpu

F32 = jnp.float32
BF16 = jnp.bfloat16

D_MODEL = 1024
N_META = 16
HG_WIDTH = 512
HG_HEADS = 4
HG_DK = 128
SB_WIDTH = 512
SB_DH = 64
SB_PAIRS = 4
N_GROUPS = 4
EXPERTS_PER_GROUP = 4
N_EXPERTS = 16
D_EXPERT = 512
DEPTH = 1
ALPHA = (2 * DEPTH) ** 0.25
LN_EPS = 1e-5
RMS_EPS = 1e-6
N_SPLITS = 7
PROJ_DTYPES = (F32, F32, F32, F32, BF16, BF16, BF16)

N_PAIRS = 6
N_CLASSES = N_GROUPS * N_PAIRS
CLS_ROWS = 32
H_ROWS = D_MODEL // 128
MOE_TILE = 512
MOE_PARTS = 2
OUT_PARTS = 2
PERM_TOKENS = 2048
ZERO_TOKENS = 64
ISSUE_UNROLL = 8

LANES = 128
HG_CHUNK = 32
HG_STEP_CHUNKS = 8
HG_SEQS = 2
SB_TILE = 128
META_TILE = 128
SKIP_LOG = -88.0
MASKED_SCORE = -1e30
VMEM_LIMIT = 56 * 1024 * 1024

NT_DIMS = (((1,), (1,)), ((), ()))


def _dot(a, b):
    return jnp.dot(a, b, preferred_element_type=F32)


def _dot_nt(a, b):
    return lax.dot_general(a, b, NT_DIMS, preferred_element_type=F32)


def _split3(x):
    hi = x.astype(BF16)
    r1 = x - hi.astype(F32)
    mid = r1.astype(BF16)
    lo = (r1 - mid.astype(F32)).astype(BF16)
    return hi, mid, lo


def _split2(x):
    hi = x.astype(BF16)
    return hi, (x - hi.astype(F32)).astype(BF16)


def _proj_kernel(x_ref, w_ref, *out_refs):
    xb = x_ref[...].astype(BF16)
    for j, o_ref in enumerate(out_refs):
        o_ref[...] = _dot(xb, w_ref[:, j * 512:(j + 1) * 512]).astype(o_ref.dtype)


def _proj(x2d, w_bf16, tm):
    n = x2d.shape[0]
    out_shape = [jax.ShapeDtypeStruct((n, 512), dt) for dt in PROJ_DTYPES]
    return pl.pallas_call(
        _proj_kernel,
        grid=(n // tm,),
        in_specs=[pl.BlockSpec((tm, D_MODEL), lambda i: (i, 0)),
                  pl.BlockSpec((D_MODEL, N_SPLITS * 512), lambda i: (0, 0))],
        out_specs=[pl.BlockSpec((tm, 512), lambda i: (i, 0)) for _ in range(N_SPLITS)],
        out_shape=out_shape,
        compiler_params=pltpu.CompilerParams(dimension_semantics=("parallel",),
                                             vmem_limit_bytes=VMEM_LIMIT),
        name="proj",
    )(x2d, w_bf16)


def _hgrn_chunk(zfs, zis, lbs, sts, tri, zqs=None, causal=None):
    n = len(zfs)
    c = zfs[0].shape[0]
    ks = [(1.0 - lbs[h]) * jax.nn.sigmoid(-zfs[h]) for h in range(n)]
    splits = [_split3(jnp.log(1.0 - k)) for k in ks]
    bcs = [_dot(tri, hi) + _dot(tri, mid) + _dot(tri, lo) for hi, mid, lo in splits]
    b_last = [bc[c - 1:c, :] for bc in bcs]
    os = None
    if zqs is not None:
        qs = [zq * jax.nn.sigmoid(zq) for zq in zqs]
        refs = [bc[c // 2 - 1:c // 2, :] for bc in bcs]
        qes = [(qs[h] * jnp.exp(bcs[h] - refs[h])).astype(BF16) for h in range(n)]
        kes = [(ks[h] * jnp.exp(refs[h] - bcs[h])).astype(BF16) for h in range(n)]
        qbs = [(qs[h] * jnp.exp(bcs[h])).astype(BF16) for h in range(n)]
        scs = [jnp.where(causal, _dot_nt(qes[h], kes[h]), 0.0).astype(BF16) for h in range(n)]
        os = [_dot(scs[h], zis[h].astype(BF16)) + _dot_nt(qbs[h], sts[h].astype(BF16)) for h in range(n)]
    kls = [(ks[h] * jnp.exp(b_last[h] - bcs[h])).astype(BF16) for h in range(n)]
    new_sts = [sts[h] * jnp.exp(b_last[h]) + _dot(zis[h].T.astype(BF16), kls[h]) for h in range(n)]
    return new_sts, os


def _hgrn_kernel(q_ref, f_ref, i_ref, g_ref, mf_ref, mi_ref, lb_ref, ng_ref, o_ref, st_ref):
    c = HG_CHUNK
    row = lax.broadcasted_iota(jnp.int32, (c, c), 0)
    col = lax.broadcasted_iota(jnp.int32, (c, c), 1)
    causal = col <= row
    tri = causal.astype(BF16)
    sls = [slice(h * HG_DK, (h + 1) * HG_DK) for h in range(HG_HEADS)]
    chains = [(s, h) for s in range(HG_SEQS) for h in range(HG_HEADS)]
    lbs = [lb_ref[:, sls[h]] for _, h in chains]

    @pl.when(pl.program_id(1) == 0)
    def _():
        sts = [jnp.zeros((HG_DK, HG_DK), F32)] * HG_HEADS
        for j in range(-(-N_META // c), 0, -1):
            rows = slice(META_TILE - j * c, META_TILE - (j - 1) * c)
            sts, _ = _hgrn_chunk([mf_ref[rows, sl] for sl in sls], [mi_ref[rows, sl] for sl in sls],
                                 lbs[:HG_HEADS], sts, tri)
        for k, (_, h) in enumerate(chains):
            st_ref[k] = sts[h]

    sts = [st_ref[k] for k in range(len(chains))]
    for j in range(HG_STEP_CHUNKS):
        rows = slice(j * c, (j + 1) * c)
        sts, os = _hgrn_chunk([f_ref[s, rows, sls[h]] for s, h in chains],
                              [i_ref[s, rows, sls[h]] for s, h in chains], lbs, sts, tri,
                              zqs=[q_ref[s, rows, sls[h]] for s, h in chains], causal=causal)
        for k, (s, h) in enumerate(chains):
            ms = jnp.mean(os[k] * os[k], axis=-1, keepdims=True)
            y = os[k] * lax.rsqrt(ms + RMS_EPS) * ng_ref[:, sls[h]] * jax.nn.sigmoid(g_ref[s, rows, sls[h]])
            o_ref[s, rows, sls[h]] = y.astype(o_ref.dtype)
    for k in range(len(chains)):
        st_ref[k] = sts[k]


def _hgrn(hq, hf, hi, hg, mf, mi, lb, ng, batch, seq):
    step_rows = HG_CHUNK * HG_STEP_CHUNKS
    blk = pl.BlockSpec((HG_SEQS, step_rows, HG_WIDTH), lambda b, c: (b, c, 0))
    meta = pl.BlockSpec((META_TILE, HG_WIDTH), lambda b, c: (0, 0))
    vec = pl.BlockSpec((1, HG_WIDTH), lambda b, c: (0, 0))
    per_seq = lambda a: a.reshape(batch, seq, HG_WIDTH)
    return pl.pallas_call(
        _hgrn_kernel,
        grid=(batch // HG_SEQS, seq // step_rows),
        in_specs=[blk, blk, blk, blk, meta, meta, vec, vec],
        out_specs=blk,
        out_shape=jax.ShapeDtypeStruct((batch, seq, HG_WIDTH), BF16),
        scratch_shapes=[pltpu.VMEM((HG_SEQS * HG_HEADS, HG_DK, HG_DK), F32)],
        compiler_params=pltpu.CompilerParams(dimension_semantics=("parallel", "arbitrary"),
                                             vmem_limit_bytes=VMEM_LIMIT),
        name="hgrn2",
    )(per_seq(hq), per_seq(hf), per_seq(hi), per_seq(hg), mf, mi, lb, ng).reshape(batch * seq, HG_WIDTH)


def _sb_tiles(qs, tiles, rev, carries):
    n = len(qs)
    chains = [(j, p) for j in range(len(tiles)) for p in range(n)]
    zs = {c: _dot_nt(qs[c[1]], tiles[c[0]][0][c[1]]) for c in chains}
    ls_pos, lks = {}, {}
    for c in chains:
        z, mask = zs[c], tiles[c[0]][2]
        if mask is not None:
            z = jnp.where(mask, z, MASKED_SCORE)
        sp = jnp.log(1.0 + jnp.exp(-jnp.abs(z)))
        ls_pos[c] = jnp.minimum(z, 0.0) - sp
        lks[c] = ls_pos[c] - z
    css = {c: _dot(jnp.concatenate(_split2(lks[c]), axis=1), rev) for c in chains}
    sums = {c: jnp.sum(lks[c], axis=-1, keepdims=True) for c in chains}
    parts = [None] * n
    carries = list(carries)
    for j, p in chains:
        a = jnp.exp(ls_pos[j, p] + css[j, p] + carries[p])
        part = _dot(a.astype(BF16), tiles[j][1][p])
        parts[p] = part if parts[p] is None else parts[p] + part
        carries[p] = carries[p] + sums[j, p]
    return parts, carries


def _sb_kernel(q_ref, k_ref, v_ref, mk_ref, mv_ref, ng_ref, o_ref, acc_ref, car_ref):
    t = SB_TILE
    qi = pl.program_id(1)
    lane = lax.broadcasted_iota(jnp.int32, (1, LANES), 1)
    lo = lane < SB_DH
    row = lax.broadcasted_iota(jnp.int32, (2 * t, t), 0)
    col = lax.broadcasted_iota(jnp.int32, (2 * t, t), 1)
    rev = (lax.broadcasted_iota(jnp.int32, (t, t), 0)
           > lax.broadcasted_iota(jnp.int32, (t, t), 1)).astype(BF16)
    rev = jnp.concatenate([rev, rev], axis=0)
    diag_mask = col < jnp.where(row >= t, row - t, row)
    meta_mask = col >= META_TILE - N_META

    def pair(ref, s, p):
        return ref[pl.ds(s, t), p * LANES:(p + 1) * LANES]

    qs = []
    for p in range(SB_PAIRS):
        q = q_ref[:, p * LANES:(p + 1) * LANES] * (SB_DH ** -0.5)
        qs.append(jnp.concatenate([jnp.where(lo, q, 0.0), jnp.where(lo, 0.0, q)], axis=0).astype(BF16))

    def x_tile(s, mask):
        return ([pair(k_ref, s, p) for p in range(SB_PAIRS)], [pair(v_ref, s, p) for p in range(SB_PAIRS)], mask)

    def meta_pair(ref, p):
        return ref[:, p * LANES:(p + 1) * LANES]

    def alive_of(carries):
        m = carries[0]
        for c in carries[1:]:
            m = jnp.maximum(m, c)
        return jnp.max(m) > SKIP_LOG

    has_prev = jnp.full((t, LANES), qi, jnp.int32) > 0
    prev = pl.multiple_of(jnp.maximum(qi - 1, 0) * t, t)
    prev_tile = ([jnp.where(has_prev, pair(k_ref, prev, p), meta_pair(mk_ref, p)) for p in range(SB_PAIRS)],
                 [jnp.where(has_prev, pair(v_ref, prev, p), meta_pair(mv_ref, p)) for p in range(SB_PAIRS)],
                 jnp.logical_or(jnp.full((2 * t, t), qi, jnp.int32) > 0, meta_mask))
    zero = jnp.zeros((2 * t, 1), F32)
    accs, carries = _sb_tiles(qs, [x_tile(pl.multiple_of(qi * t, t), diag_mask), prev_tile], rev,
                              [zero] * SB_PAIRS)

    def cond(state):
        return jnp.logical_and(state[0] >= 0, state[1])

    def body(state):
        kt, _, accs, carries = state
        parts, carries = _sb_tiles(qs, [x_tile(pl.multiple_of(kt * t, t), None)], rev, carries)
        return kt - 1, alive_of(carries), [a + b for a, b in zip(accs, parts)], carries

    _, alive, accs, carries = lax.while_loop(cond, body, (qi - 2, alive_of(carries), accs, carries))
    for p in range(SB_PAIRS):
        acc_ref[p] = accs[p]
        car_ref[p] = carries[p]

    @pl.when(jnp.logical_and(alive, qi > 0))
    def _():
        meta_tile = ([meta_pair(mk_ref, p) for p in range(SB_PAIRS)],
                     [meta_pair(mv_ref, p) for p in range(SB_PAIRS)], meta_mask)
        parts, _ = _sb_tiles(qs, [meta_tile], rev, [car_ref[p] for p in range(SB_PAIRS)])
        for p in range(SB_PAIRS):
            acc_ref[p] += parts[p]

    for p in range(SB_PAIRS):
        sl = slice(p * LANES, (p + 1) * LANES)
        o = jnp.where(lo, acc_ref[p, :t, :], acc_ref[p, t:, :])
        sq = o * o
        ms_lo = jnp.sum(jnp.where(lo, sq, 0.0), axis=-1, keepdims=True) * (1.0 / SB_DH)
        ms_hi = jnp.sum(jnp.where(lo, 0.0, sq), axis=-1, keepdims=True) * (1.0 / SB_DH)
        inv = jnp.where(lo, lax.rsqrt(ms_lo + RMS_EPS), lax.rsqrt(ms_hi + RMS_EPS))
        o_ref[:, sl] = (o * inv * ng_ref[:, sl]).astype(o_ref.dtype)


def _stickbreak(sq, sk, sv, mk, mv, ng, batch, seq):
    nq = seq // SB_TILE
    qblk = pl.BlockSpec((SB_TILE, SB_WIDTH), lambda b, i: (b * nq + i, 0))
    kvblk = pl.BlockSpec((seq, SB_WIDTH), lambda b, i: (b, 0))
    mblk = pl.BlockSpec((META_TILE, SB_WIDTH), lambda b, i: (0, 0))
    gblk = pl.BlockSpec((1, SB_WIDTH), lambda b, i: (0, 0))
    return pl.pallas_call(
        _sb_kernel,
        grid=(batch, nq),
        in_specs=[qblk, kvblk, kvblk, mblk, mblk, gblk],
        out_specs=qblk,
        out_shape=jax.ShapeDtypeStruct((batch * seq, SB_WIDTH), BF16),
        scratch_shapes=[pltpu.VMEM((SB_PAIRS, 2 * SB_TILE, LANES), F32),
                        pltpu.VMEM((SB_PAIRS, 2 * SB_TILE, 1), F32)],
        compiler_params=pltpu.CompilerParams(dimension_semantics=("parallel", "arbitrary"),
                                             vmem_limit_bytes=VMEM_LIMIT),
        name="stickbreak",
    )(sq, sk, sv, mk, mv, ng)


def _layer_norm(x, g, b):
    mu = jnp.mean(x, axis=-1, keepdims=True)
    xc = x - mu
    var = jnp.mean(xc * xc, axis=-1, keepdims=True)
    return xc * lax.rsqrt(var + LN_EPS) * g + b


def _argmax_first(rows):
    best = rows[0]
    idx = jnp.zeros(best.shape, jnp.int32)
    for j in range(1, len(rows)):
        better = rows[j] > best
        idx = jnp.where(better, j, idx)
        best = jnp.where(better, rows[j], best)
    return best, idx


def _router_logits(h1, wr_ref):
    h_hi, h_mid = _split2(h1)
    both = _dot(h_hi, wr_ref[...])
    return both[:, :LANES] + both[:, LANES:] + _dot(h_mid, wr_ref[:, :LANES])


def _outproj_kernel(x_ref, ohg_ref, osb_ref, w_ref, g_ref, b_ref, wr_ref, br_ref,
                    hx_ref, cls_ref, rank_ref, cnt_ref, run_ref):
    tm = x_ref.shape[0]

    @pl.when(pl.program_id(0) == 0)
    def _():
        run_ref[...] = jnp.zeros(run_ref.shape, F32)

    rp = tm // OUT_PARTS
    half = D_MODEL // 2
    rows = [slice(m * rp, (m + 1) * rp) for m in range(OUT_PARTS)]
    mixed = [jnp.concatenate([ohg_ref[r, :], osb_ref[r, :]], axis=1) for r in rows]
    ys = [jnp.concatenate([_dot(mx, w_ref[:, :half]), _dot(mx, w_ref[:, half:])], axis=1) for mx in mixed]
    h1s = [_layer_norm(ALPHA * x_ref[r, :] + y, g_ref[...], b_ref[...]) for r, y in zip(rows, ys)]
    for m, h1 in enumerate(h1s):
        for s in range(H_ROWS):
            hx_ref[pl.ds(m * rp * H_ROWS + s, rp, stride=H_ROWS), :] = h1[:, s * LANES:(s + 1) * LANES]

    lt = jnp.concatenate([_router_logits(h1, wr_ref).T for h1 in h1s], axis=1) + br_ref[...]
    _, grp = _argmax_first([lt[g:g + 1, :] for g in range(N_GROUPS)])

    li = []
    for e in range(EXPERTS_PER_GROUP):
        acc = jnp.where(grp == 0, lt[N_GROUPS + e:N_GROUPS + e + 1, :], 0.0)
        for g in range(1, N_GROUPS):
            r = N_GROUPS + g * EXPERTS_PER_GROUP + e
            acc = acc + jnp.where(grp == g, lt[r:r + 1, :], 0.0)
        li.append(acc)
    v1, i1 = _argmax_first(li)
    neg = jnp.full_like(v1, -jnp.inf)
    _, i2 = _argmax_first([jnp.where(i1 == e, neg, li[e]) for e in range(EXPERTS_PER_GROUP)])

    e_lo = jnp.minimum(i1, i2)
    e_hi = jnp.maximum(i1, i2)
    pair = jnp.where(e_lo == 0, e_hi - 1, jnp.where(e_lo == 1, e_hi + 1, 5))
    cls = grp * N_PAIRS + pair

    onehot = lax.broadcasted_iota(jnp.int32, (CLS_ROWS, tm), 0) == cls
    oh = onehot.astype(BF16)
    earlier = (lax.broadcasted_iota(jnp.int32, (tm, tm), 0)
               < lax.broadcasted_iota(jnp.int32, (tm, tm), 1)).astype(BF16)
    run = run_ref[...]
    rank = jnp.sum(jnp.where(onehot, _dot(oh, earlier) + run, 0.0), axis=0, keepdims=True)
    cls_ref[...] = cls
    rank_ref[...] = rank.astype(jnp.int32)
    run = run + jnp.sum(oh.astype(F32), axis=1, keepdims=True)
    run_ref[...] = run
    cnt_ref[...] = jnp.broadcast_to(run, cnt_ref.shape)


def _outproj(x2d, ohg, osb, w_bf16, g, b, wr_split, br, tm):
    n = x2d.shape[0]
    const = lambda shape: pl.BlockSpec(shape, lambda i: (0, 0))
    return pl.pallas_call(
        _outproj_kernel,
        grid=(n // tm,),
        in_specs=[pl.BlockSpec((tm, D_MODEL), lambda i: (i, 0)),
                  pl.BlockSpec((tm, HG_WIDTH), lambda i: (i, 0)),
                  pl.BlockSpec((tm, SB_WIDTH), lambda i: (i, 0)),
                  const((D_MODEL, D_MODEL)), const((1, D_MODEL)), const((1, D_MODEL)),
                  const((D_MODEL, 2 * LANES)), const((LANES, 1))],
        out_specs=[pl.BlockSpec((tm * H_ROWS, LANES), lambda i: (i, 0)),
                   pl.BlockSpec((1, tm), lambda i: (0, i)),
                   pl.BlockSpec((1, tm), lambda i: (0, i)),
                   const((CLS_ROWS, LANES))],
        out_shape=[jax.ShapeDtypeStruct((n * H_ROWS, LANES), F32),
                   jax.ShapeDtypeStruct((1, n), jnp.int32),
                   jax.ShapeDtypeStruct((1, n), jnp.int32),
                   jax.ShapeDtypeStruct((CLS_ROWS, LANES), F32)],
        scratch_shapes=[pltpu.VMEM((CLS_ROWS, 1), F32)],
        compiler_params=pltpu.CompilerParams(dimension_semantics=("arbitrary",),
                                             vmem_limit_bytes=VMEM_LIMIT),
        name="outproj",
    )(x2d, ohg, osb, w_bf16, g, b, wr_split, br)


def _slab(ref, token, rows):
    return ref.at[pl.ds(pl.multiple_of(token * rows, rows), rows)]


def _scatter_kernel(pos_ref, zfrom_ref, zto_ref, src_ref, dst_ref, zero_ref, sem, zsem):
    base = pl.program_id(0) * PERM_TOKENS

    @pl.when(pl.program_id(0) == 0)
    def _():
        zero_ref[...] = jnp.zeros(zero_ref.shape, F32)

        def fill(k):
            return pltpu.make_async_copy(zero_ref, _slab(dst_ref, k, ZERO_TOKENS * H_ROWS), zsem)

        def each_chunk(do):
            def per_range(c, carry):
                def per_chunk(k, carry):
                    do(fill(k))
                    return carry
                return lax.fori_loop(zfrom_ref[c], zto_ref[c], per_chunk, carry)
            lax.fori_loop(0, zfrom_ref.shape[0], per_range, 0)

        each_chunk(lambda copy: copy.start())
        each_chunk(lambda copy: copy.wait())

    def issue(g, carry):
        for u in range(ISSUE_UNROLL):
            r = g * ISSUE_UNROLL + u
            pltpu.make_async_copy(_slab(src_ref, r, H_ROWS), _slab(dst_ref, pos_ref[base + r], H_ROWS),
                                  sem).start(priority=u % 2)
        return carry

    lax.fori_loop(0, PERM_TOKENS // ISSUE_UNROLL, issue, 0)
    pltpu.make_async_copy(src_ref, dst_ref.at[pl.ds(0, PERM_TOKENS * H_ROWS)], sem).wait()


def _scatter_tokens(pos, zfrom, zto, hx, sorted_tokens):
    n = pos.shape[0]
    return pl.pallas_call(
        _scatter_kernel,
        grid_spec=pltpu.PrefetchScalarGridSpec(
            num_scalar_prefetch=3,
            grid=(n // PERM_TOKENS,),
            in_specs=[pl.BlockSpec((PERM_TOKENS * H_ROWS, LANES), lambda i, pos, zf, zt: (i, 0))],
            out_specs=pl.BlockSpec(memory_space=pl.ANY),
            scratch_shapes=[pltpu.VMEM((ZERO_TOKENS * H_ROWS, LANES), F32),
                            pltpu.SemaphoreType.DMA(()), pltpu.SemaphoreType.DMA(())]),
        out_shape=jax.ShapeDtypeStruct((sorted_tokens * H_ROWS, LANES), F32),
        compiler_params=pltpu.CompilerParams(dimension_semantics=("arbitrary",),
                                             vmem_limit_bytes=VMEM_LIMIT),
        name="scatter_tokens",
    )(pos, zfrom, zto, hx)


def _gather_kernel(pos_ref, src_ref, o_ref, buf_ref, sem):
    i = pl.program_id(0)
    last = pl.num_programs(0) - 1
    slot = i % 2
    groups = PERM_TOKENS // ISSUE_UNROLL

    def issue(g):
        for u in range(ISSUE_UNROLL):
            r = g * ISSUE_UNROLL + u
            pltpu.make_async_copy(_slab(src_ref, pos_ref[i * PERM_TOKENS + r], H_ROWS),
                                  _slab(buf_ref.at[slot], r, H_ROWS), sem.at[slot]).start(priority=u % 2)

    def relayout(g):
        rows = pl.ds(pl.multiple_of(g * ISSUE_UNROLL, ISSUE_UNROLL), ISSUE_UNROLL)
        for s in range(H_ROWS):
            o_ref[rows, s * LANES:(s + 1) * LANES] = buf_ref[
                1 - slot, pl.ds(g * ISSUE_UNROLL * H_ROWS + s, ISSUE_UNROLL, stride=H_ROWS), :]

    def wait_previous():
        pltpu.make_async_copy(src_ref.at[pl.ds(0, PERM_TOKENS * H_ROWS)], buf_ref.at[1 - slot],
                              sem.at[1 - slot]).wait()

    def loop(body):
        def step(g, carry):
            body(g)
            return carry
        lax.fori_loop(0, groups, step, 0)

    @pl.when(i == 0)
    def _():
        loop(issue)

    @pl.when(jnp.logical_and(i > 0, i < last))
    def _():
        wait_previous()
        loop(lambda g: (issue(g), relayout(g)))

    @pl.when(i == last)
    def _():
        wait_previous()
        loop(relayout)


def _gather_tokens(pos, ys):
    n = pos.shape[0]
    return pl.pallas_call(
        _gather_kernel,
        grid_spec=pltpu.PrefetchScalarGridSpec(
            num_scalar_prefetch=1,
            grid=(n // PERM_TOKENS + 1,),
            in_specs=[pl.BlockSpec(memory_space=pl.ANY)],
            out_specs=pl.BlockSpec((PERM_TOKENS, D_MODEL), lambda i, pos: (jnp.maximum(i - 1, 0), 0)),
            scratch_shapes=[pltpu.VMEM((2, PERM_TOKENS * H_ROWS, LANES), F32), pltpu.SemaphoreType.DMA((2,))]),
        out_shape=jax.ShapeDtypeStruct((n, D_MODEL), F32),
        compiler_params=pltpu.CompilerParams(dimension_semantics=("arbitrary",),
                                             vmem_limit_bytes=VMEM_LIMIT),
        name="gather_tokens",
    )(pos, ys)


def _moe_kernel(ea_ref, eb_ref, rows_ref, x_ref, w1a_ref, w3a_ref, w2a_ref, w1b_ref, w3b_ref, w2b_ref,
                wr_ref, br_ref, g_ref, b_ref, o_ref):
    i = pl.program_id(0)
    t = MOE_TILE

    @pl.when(rows_ref[i] > 0)
    def _():
        rp = t // MOE_PARTS
        lane = lax.broadcasted_iota(jnp.int32, (1, LANES), 1)
        pick = lambda lg, j: jnp.sum(jnp.where(lane == j, lg, 0.0), axis=-1, keepdims=True)
        parts = range(MOE_PARTS)
        h1s = [jnp.concatenate([x_ref[pl.ds(m * rp * H_ROWS + s, rp, stride=H_ROWS), :] for s in range(H_ROWS)],
                               axis=1) for m in parts]
        hbs = [h1.astype(BF16) for h1 in h1s]

        gates = []
        for h1 in h1s:
            lg = _router_logits(h1, wr_ref) + br_ref[...]
            l_grp = pick(lg, ea_ref[i] // EXPERTS_PER_GROUP)
            l_a, l_b = pick(lg, N_GROUPS + ea_ref[i]), pick(lg, N_GROUPS + eb_ref[i])
            p_grp = 1.0 / jnp.sum(jnp.where(lane < N_GROUPS, jnp.exp(lg - l_grp), 0.0), axis=-1, keepdims=True)
            gates.append((p_grp / (1.0 + jnp.exp(l_b - l_a)), p_grp / (1.0 + jnp.exp(l_a - l_b))))

        def hidden(hb, w1_ref, w3_ref, gate):
            a = _dot(hb, w1_ref[0])
            return (gate * (a * jax.nn.sigmoid(a)) * _dot(hb, w3_ref[0])).astype(BF16)

        hid_a = [hidden(hbs[m], w1a_ref, w3a_ref, gates[m][0]) for m in parts]
        hid_b = [hidden(hbs[m], w1b_ref, w3b_ref, gates[m][1]) for m in parts]
        ys = [_dot(hid_a[m], w2a_ref[0]) + _dot(hid_b[m], w2b_ref[0]) for m in parts]
        for m in parts:
            out = _layer_norm(ALPHA * h1s[m] + ys[m], g_ref[...], b_ref[...])
            for s in range(H_ROWS):
                o_ref[pl.ds(m * rp * H_ROWS + s, rp, stride=H_ROWS), :] = out[:, s * LANES:(s + 1) * LANES]

    @pl.when(rows_ref[i] == 0)
    def _():
        o_ref[...] = jnp.zeros(o_ref.shape, F32)


def _moe(tile_ea, tile_eb, tile_rows, xs, w1, w3, w2, wr, br_row, g, b):
    nt = tile_ea.shape[0]
    wa = lambda shape: pl.BlockSpec(shape, lambda i, ea, eb, r: (ea[i], 0, 0))
    wb = lambda shape: pl.BlockSpec(shape, lambda i, ea, eb, r: (eb[i], 0, 0))
    up, down = (1, D_MODEL, D_EXPERT), (1, D_EXPERT, D_MODEL)
    return pl.pallas_call(
        _moe_kernel,
        grid_spec=pltpu.PrefetchScalarGridSpec(
            num_scalar_prefetch=3,
            grid=(nt,),
            in_specs=[pl.BlockSpec((MOE_TILE * H_ROWS, LANES), lambda i, ea, eb, r: (i, 0)),
                      wa(up), wa(up), wa(down), wb(up), wb(up), wb(down),
                      pl.BlockSpec((D_MODEL, 2 * LANES), lambda i, ea, eb, r: (0, 0)),
                      pl.BlockSpec((1, LANES), lambda i, ea, eb, r: (0, 0)),
                      pl.BlockSpec((1, D_MODEL), lambda i, ea, eb, r: (0, 0)),
                      pl.BlockSpec((1, D_MODEL), lambda i, ea, eb, r: (0, 0))],
            out_specs=pl.BlockSpec((MOE_TILE * H_ROWS, LANES), lambda i, ea, eb, r: (i, 0))),
        out_shape=jax.ShapeDtypeStruct((nt * MOE_TILE * H_ROWS, LANES), F32),
        compiler_params=pltpu.CompilerParams(dimension_semantics=("arbitrary",),
                                             vmem_limit_bytes=VMEM_LIMIT),
        name="moe",
    )(tile_ea, tile_eb, tile_rows, xs, w1, w3, w2, w1, w3, w2, wr, br_row, g, b)


def _route(cls, rank, counts, n):
    cnt = counts[:N_CLASSES, 0].astype(jnp.int32)
    ntile = (cnt + MOE_TILE - 1) // MOE_TILE
    tile_end = jnp.cumsum(ntile)
    tile_start = tile_end - ntile
    pos = (tile_start * MOE_TILE)[cls.reshape(n)] + rank.reshape(n)
    tiles = jnp.arange(n // MOE_TILE + N_CLASSES, dtype=jnp.int32)
    tcls = jnp.minimum(jnp.sum((tiles[:, None] >= tile_end[None, :]).astype(jnp.int32), axis=1), N_CLASSES - 1)
    rows = jnp.clip(cnt[tcls] - (tiles - tile_start[tcls]) * MOE_TILE, 0, MOE_TILE)
    rows = jnp.where(tiles < tile_end[-1], rows, 0)
    pair_lo = jnp.array([0, 0, 0, 1, 1, 2], jnp.int32)
    pair_hi = jnp.array([1, 2, 3, 2, 3, 3], jnp.int32)
    grp, pair = tcls // N_PAIRS, tcls % N_PAIRS
    per_tile = MOE_TILE // ZERO_TOKENS
    zfrom = jnp.concatenate([(tile_start * MOE_TILE + cnt) // ZERO_TOKENS, tile_end[-1:] * per_tile])
    zto = jnp.concatenate([tile_end * per_tile, jnp.full((1,), tiles.shape[0] * per_tile, jnp.int32)])
    return (pos, grp * EXPERTS_PER_GROUP + pair_lo[pair], grp * EXPERTS_PER_GROUP + pair_hi[pair], rows,
            zfrom.astype(jnp.int32), zto.astype(jnp.int32))


def kernel(x, meta_tokens, w_in, hg_lower_bound, hg_norm_g, sb_norm_g, w_out, ln1_g, ln1_b,
           w_router_group, b_router_group, w_router_expert, b_router_expert,
           w_exp_gate, w_exp_up, w_exp_down, ln2_g, ln2_b):
    batch, seq, d = x.shape
    assert d == D_MODEL and seq % (HG_CHUNK * HG_STEP_CHUNKS) == 0 and seq % SB_TILE == 0
    assert w_in.shape[0] == DEPTH == 1 and batch % HG_SEQS == 0
    n = batch * seq
    assert n % PERM_TOKENS == 0 and n % MOE_TILE == 0
    x2d = x.reshape(n, d)

    w_in_b = w_in[0].astype(BF16)
    w_out_b = w_out[0].astype(BF16)
    lb = jnp.cumsum(jax.nn.softmax(hg_lower_bound.astype(F32), axis=0), axis=0)[0].reshape(1, HG_WIDTH)
    n_logits = N_GROUPS + N_EXPERTS
    wr = jnp.concatenate([w_router_group[0], w_router_expert[0].reshape(d, N_EXPERTS),
                          jnp.zeros((d, LANES - n_logits), F32)], axis=1)
    wr_hi = wr.astype(BF16)
    wr_split = jnp.concatenate([wr_hi, (wr - wr_hi.astype(F32)).astype(BF16)], axis=1)
    br = jnp.concatenate([b_router_group[0], b_router_expert[0].reshape(N_EXPERTS),
                          jnp.zeros((LANES - n_logits,), F32)]).reshape(LANES, 1)
    meta_pad = jnp.concatenate([jnp.zeros((META_TILE - N_META, d), F32), meta_tokens.astype(F32)], axis=0)

    hq, hf, hi, hg, sq, sk, sv = _proj(x2d, w_in_b, 512)
    _, mf, mi, _, _, mk, mv = _proj(meta_pad, w_in_b, META_TILE)

    o_hg = _hgrn(hq, hf, hi, hg, mf, mi, lb, hg_norm_g[0].reshape(1, HG_WIDTH), batch, seq)
    o_sb = _stickbreak(sq, sk, sv, mk, mv, sb_norm_g[0].reshape(1, SB_WIDTH), batch, seq)

    hx, cls, rank, counts = _outproj(x2d, o_hg, o_sb, w_out_b, ln1_g[0].reshape(1, d), ln1_b[0].reshape(1, d),
                                     wr_split, br, 512)
    pos, tile_ea, tile_eb, tile_rows, zfrom, zto = _route(cls, rank, counts, n)
    xs = _scatter_tokens(pos, zfrom, zto, hx, tile_ea.shape[0] * MOE_TILE)
    ys = _moe(tile_ea, tile_eb, tile_rows, xs, w_exp_gate[0].astype(BF16), w_exp_up[0].astype(BF16),
              w_exp_down[0].astype(BF16), wr_split, br.reshape(1, LANES), ln2_g[0].reshape(1, d),
              ln2_b[0].reshape(1, d))
    return _gather_tokens(pos, ys).reshape(batch, seq, d)
```
